```python
import math
import jax, jax.numpy as jnp
from jax import lax
import numpy as np

D_MODEL = 4096
BATCH = 16
SEQ = 256
DEPTH = 4
DEC_BATCH = 2
DEC_SEQ = 1024
PAST_LEN = 512

GRID_W = 64
N_EVEN = (DEPTH + 1) // 2
N_ODD = DEPTH // 2
N_MOD = 6
EPS = 1e-6

SSD_INNER = D_MODEL
SSD_HEAD_DIM = 64
SSD_HEADS = SSD_INNER // SSD_HEAD_DIM
SSD_GROUPS = 8
SSD_STATE = 128
SSD_CONV = 5
SSD_CHUNK = 128
SSD_CONV_DIM = SSD_INNER + 2 * SSD_GROUPS * SSD_STATE

CFM_DIM = D_MODEL
CFM_KERNEL = 31

EVEN_IN = SSD_INNER + SSD_CONV_DIM + 2 * SSD_HEADS + 2 * CFM_DIM
EVEN_OUT = SSD_INNER + CFM_DIM

GM_DIM = 2 * D_MODEL
GM_GROUPS = 16
GM_CHUNK = 128

D_FF = 11008
N_EXPERTS = 8
TOP_K = 2
D_FF_EXPERT = 7168

kernel_name = 'hybrid_ssd_conformer_gmlp_diffusion_step'


def rmsnorm(x, g):
    xf = x.astype(jnp.float32)
    y = xf * lax.rsqrt(jnp.mean(xf * xf, axis=-1, keepdims=True) + EPS)
    return y.astype(x.dtype) * g


def layernorm(x, g, b):
    xf = x.astype(jnp.float32)
    mu = jnp.mean(xf, axis=-1, keepdims=True)
    var = jnp.mean(jnp.square(xf - mu), axis=-1, keepdims=True)
    return ((xf - mu) * lax.rsqrt(var + EPS)).astype(x.dtype) * g + b


def dwconv(x, w, b):
    pad = w.shape[0] // 2
    y = lax.conv_general_dilated(x, w[:, None, :].astype(x.dtype), (1,), [(pad, pad)],
                                 dimension_numbers=('NWC', 'WIO', 'NWC'),
                                 feature_group_count=x.shape[-1])
    return y + b


def grid_pos_embed(n_tokens, dim, dtype):
    rows = n_tokens // GRID_W
    quarter = dim // 4
    omega = 1.0 / (10000.0 ** (jnp.arange(quarter, dtype=jnp.float32) / quarter))

    def axis_embed(n):
        ang = jnp.arange(n, dtype=jnp.float32)[:, None] * omega[None, :]
        return jnp.concatenate([jnp.sin(ang), jnp.cos(ang)], axis=-1)

    er = jnp.broadcast_to(axis_embed(rows)[:, None, :], (rows, GRID_W, dim // 2))
    ec = jnp.broadcast_to(axis_embed(GRID_W)[None, :, :], (rows, GRID_W, dim // 2))
    return jnp.concatenate([er, ec], axis=-1).reshape(n_tokens, dim).astype(dtype)


def ssd_scan(x, dt, a, bm, cm, h0):
    b, L, H, P = x.shape
    G, N = bm.shape[-2:]
    R = H // G
    T = SSD_CHUNK
    nc = L // T
    f32 = jnp.float32
    dtf = dt.astype(f32)
    xdt = (x.astype(f32) * dtf[..., None]).reshape(b, nc, T, G, R, P)
    da = (dtf * a.astype(f32)).reshape(b, nc, T, G, R)
    cs = jnp.cumsum(jnp.moveaxis(da, 2, -1), axis=-1)
    bc = bm.astype(f32).reshape(b, nc, T, G, N)
    cc = cm.astype(f32).reshape(b, nc, T, G, N)
    lower = jnp.tril(jnp.ones((T, T), dtype=bool))
    seg = jnp.exp(jnp.where(lower, cs[..., :, None] - cs[..., None, :], -jnp.inf))
    cb = jnp.einsum('bclgn,bcsgn->bcgls', cc, bc)
    y_diag = jnp.einsum('bcgrls,bcsgrp->bclgrp', cb[:, :, :, None] * seg, xdt)
    states = jnp.einsum('bcsgn,bcgrs,bcsgrp->bcgrpn', bc, jnp.exp(cs[..., -1:] - cs), xdt)

    def step(h, inp):
        decay, st = inp
        return decay[..., None, None] * h + st, h

    h_last, h_prev = lax.scan(step, h0.astype(f32).reshape(b, G, R, P, N),
                              (jnp.moveaxis(jnp.exp(cs[..., -1]), 1, 0), jnp.moveaxis(states, 1, 0)))
    y_off = jnp.einsum('bclgn,cbgrpn,bcgrl->bclgrp', cc, h_prev, jnp.exp(cs))
    y = (y_diag + y_off).reshape(b, L, H, P).astype(x.dtype)
    return y, h_last.reshape(b, H, P, N)


def even_mixer(h, h0, w_in, conv_w, conv_b, dt_bias, a_log, d_skip, ssd_norm,
               cfm_w, cfm_b, cfm_g, cfm_beta, w_out):
    b, L, _ = h.shape
    proj = h @ w_in
    z, xbc, dt, glu = jnp.split(
        proj, [SSD_INNER, SSD_INNER + SSD_CONV_DIM, SSD_INNER + SSD_CONV_DIM + 2 * SSD_HEADS], axis=-1)
    xbc = jax.nn.silu(dwconv(xbc, conv_w, conv_b))
    xs, bm, cm = jnp.split(xbc, [SSD_INNER, SSD_INNER + SSD_GROUPS * SSD_STATE], axis=-1)
    xs = xs.reshape(b, L, SSD_HEADS, SSD_HEAD_DIM)
    bm = bm.reshape(b, L, SSD_GROUPS, SSD_STATE)
    cm = cm.reshape(b, L, SSD_GROUPS, SSD_STATE)
    dt = jax.nn.softplus((dt.reshape(b, L, 2, SSD_HEADS) + dt_bias).astype(jnp.float32))
    a = -jnp.exp(a_log.astype(jnp.float32))
    y_f, s_f = ssd_scan(xs, dt[:, :, 0], a[0], bm, cm, h0[:, 0])
    rev = lambda t: jnp.flip(t, axis=1)
    y_b, s_b = ssd_scan(rev(xs), rev(dt[:, :, 1]), a[1], rev(bm), rev(cm), h0[:, 1])
    y = y_f + rev(y_b) + xs * d_skip[:, None]
    y = rmsnorm(y.reshape(b, L, SSD_INNER) * jax.nn.silu(z), ssd_norm)
    ga, gb = jnp.split(glu, 2, axis=-1)
    u = ga * jax.nn.sigmoid(gb)
    u = jax.nn.silu(layernorm(dwconv(u, cfm_w, cfm_b), cfm_g, cfm_beta))
    out = jnp.concatenate([y, u], axis=-1) @ w_out
    return out, jnp.stack([s_f, s_b], axis=1)


def odd_mixer(h, w_in, ln_g, ln_b, w_s, b_s, w_out):
    b, L, _ = h.shape
    u, v = jnp.split(jax.nn.gelu(h @ w_in), 2, axis=-1)
    v = layernorm(v, ln_g, ln_b).reshape(b, L // GM_CHUNK, GM_CHUNK, GM_GROUPS, GM_DIM // GM_GROUPS)
    v = jnp.einsum('gts,bcsgk->bctgk', w_s, v) + b_s.T[:, :, None]
    return (u * v.reshape(b, L, GM_DIM)) @ w_out


def swiglu(h, w1, w3, w2):
    return (jax.nn.silu(h @ w1) * (h @ w3)) @ w2


def moe_swiglu(h, router_w, w1, w3, w2):
    b, L, D = h.shape
    t = h.reshape(b * L, D)
    logits = (t @ router_w).astype(jnp.float32)
    top_v, top_i = lax.top_k(logits, TOP_K)
    gates = jax.nn.softmax(top_v, axis=-1)
    combine = jnp.sum(jax.nn.one_hot(top_i, N_EXPERTS, dtype=jnp.float32) * gates[..., None],
                      axis=1).astype(h.dtype)
    out = jnp.zeros_like(t)
    for e in range(N_EXPERTS):
        out = out + combine[:, e:e + 1] * swiglu(t, w1[e], w3[e], w2[e])
    return out.reshape(b, L, D)


def setup_inputs(seed: int = 0) -> dict:
    key = jax.random.key(seed)
    ks = iter(jax.random.split(key, 48))
    f32 = jnp.float32
    D = D_MODEL

    def nrm(shape, scale):
        return jax.random.normal(next(ks), shape, f32) * scale

    def gain(shape, s=0.02):
        return 1.0 + nrm(shape, s)

    dt0 = jnp.exp(jax.random.uniform(next(ks), (N_EVEN, 2, SSD_HEADS), f32,
                                     math.log(1e-3), math.log(1e-1)))
    a_log = jnp.log(jax.random.uniform(next(ks), (N_EVEN, 2, SSD_HEADS), f32, 1.0, 16.0))
    return {
        'x_prompt': nrm((BATCH, SEQ, D), 1.0),
        'x_sample': nrm((DEC_BATCH, DEC_SEQ, D), 1.0),
        'state_ssd': nrm((DEC_BATCH, N_EVEN, 2, SSD_HEADS, SSD_HEAD_DIM, SSD_STATE), 0.3),
        'c': nrm((DEC_BATCH, D), 1.0),
        'c_ctx': nrm((D,), 1.0),
        'w_mod': nrm((DEPTH, D, N_MOD * D), 0.5 * D ** -0.5),
        'b_mod': nrm((DEPTH, N_MOD * D), 0.02),
        'norm_mix_pre': gain((DEPTH, D)),
        'norm_mix_post': gain((DEPTH, D)),
        'norm_ffn_pre': gain((DEPTH, D)),
        'norm_ffn_post': gain((DEPTH, D)),
        'w_in_even': nrm((N_EVEN, D, EVEN_IN), D ** -0.5),
        'ssd_conv_w': nrm((N_EVEN, SSD_CONV, SSD_CONV_DIM), SSD_CONV ** -0.5),
        'ssd_conv_b': nrm((N_EVEN, SSD_CONV_DIM), 0.02),
        'ssd_dt_bias': dt0 + jnp.log(-jnp.expm1(-dt0)),
        'ssd_a_log': a_log,
        'ssd_d': gain((N_EVEN, SSD_HEADS), 0.1),
        'ssd_norm': gain((N_EVEN, SSD_INNER)),
        'cfm_conv_w': nrm((N_EVEN, CFM_KERNEL, CFM_DIM), CFM_KERNEL ** -0.5),
        'cfm_conv_b': nrm((N_EVEN, CFM_DIM), 0.02),
        'cfm_ln_g': gain((N_EVEN, CFM_DIM)),
        'cfm_ln_b': nrm((N_EVEN, CFM_DIM), 0.02),
        'w_out_even': nrm((N_EVEN, EVEN_OUT, D), EVEN_OUT ** -0.5),
        'w_in_odd': nrm((N_ODD, D, 2 * GM_DIM), D ** -0.5),
        'gm_ln_g': gain((N_ODD, GM_DIM)),
        'gm_ln_b': nrm((N_ODD, GM_DIM), 0.02),
        'gm_w_s': nrm((N_ODD, GM_GROUPS, GM_CHUNK, GM_CHUNK), GM_CHUNK ** -0.5),
        'gm_b_s': gain((N_ODD, GM_GROUPS, GM_CHUNK)),
        'w_out_odd': nrm((N_ODD, GM_DIM, D), GM_DIM ** -0.5),
        'ffn_w1': nrm((N_EVEN, D, D_FF), D ** -0.5),
        'ffn_w3': nrm((N_EVEN, D, D_FF), D ** -0.5),
        'ffn_w2': nrm((N_EVEN, D_FF, D), D_FF ** -0.5),
        'router_w': nrm((N_ODD, D, N_EXPERTS), D ** -0.5),
        'moe_w1': nrm((N_ODD, N_EXPERTS, D, D_FF_EXPERT), D ** -0.5),
        'moe_w3': nrm((N_ODD, N_EXPERTS, D, D_FF_EXPERT), D ** -0.5),
        'moe_w2': nrm((N_ODD, N_EXPERTS, D_FF_EXPERT, D), D_FF_EXPERT ** -0.5),
    }


def reference(x_prompt, x_sample, state_ssd, c, c_ctx, w_mod, b_mod, norm_mix_pre, norm_mix_post,
              norm_ffn_pre, norm_ffn_post, w_in_even, ssd_conv_w, ssd_conv_b, ssd_dt_bias, ssd_a_log,
              ssd_d, ssd_norm, cfm_conv_w, cfm_conv_b, cfm_ln_g, cfm_ln_b, w_out_even, w_in_odd,
              gm_ln_g, gm_ln_b, gm_w_s, gm_b_s, w_out_odd, ffn_w1, ffn_w3, ffn_w2, router_w,
              moe_w1, moe_w3, moe_w2):
    def trunk(x, cond, h0_all):
        silu_cond = jax.nn.silu(cond)
        states = []
        for l in range(DEPTH):
            i = l // 2
            mod = (silu_cond @ w_mod[l] + b_mod[l])[:, None, :]
            sh1, sc1, g1, sh2, sc2, g2 = jnp.split(mod, N_MOD, axis=-1)
            h = rmsnorm(x, norm_mix_pre[l]) * (1.0 + sc1) + sh1
            if l % 2 == 0:
                y, s = even_mixer(h, h0_all[:, i], w_in_even[i], ssd_conv_w[i], ssd_conv_b[i],
                                  ssd_dt_bias[i], ssd_a_log[i], ssd_d[i], ssd_norm[i], cfm_conv_w[i],
                                  cfm_conv_b[i], cfm_ln_g[i], cfm_ln_b[i], w_out_even[i])
                states.append(s)
            else:
                y = odd_mixer(h, w_in_odd[i], gm_ln_g[i], gm_ln_b[i], gm_w_s[i], gm_b_s[i], w_out_odd[i])
            x = x + g1 * rmsnorm(y, norm_mix_post[l])
            h = rmsnorm(x, norm_ffn_pre[l]) * (1.0 + sc2) + sh2
            if l % 2 == 0:
                y = swiglu(h, ffn_w1[i], ffn_w3[i], ffn_w2[i])
            else:
                y = moe_swiglu(h, router_w[i], moe_w1[i], moe_w3[i], moe_w2[i])
            x = x + g2 * rmsnorm(y, norm_ffn_post[l])
        return x, jnp.stack(states, axis=1)

    h0_ctx = jnp.zeros((x_prompt.shape[0],) + state_ssd.shape[1:], x_prompt.dtype)
    y_prompt, state_new = trunk(x_prompt, c_ctx[None, :], h0_ctx)
    x_lat = x_sample + grid_pos_embed(x_sample.shape[1], x_sample.shape[2], x_sample.dtype)
    y_sample, _ = trunk(x_lat, c, state_ssd)
    state_ssd_new = state_new.astype(state_ssd.dtype)
    return (y_prompt, y_sample, state_ssd_new)
```

```python
import functools
import math

import jax
import jax.numpy as jnp
from jax import lax
from jax.experimental import pallas as pl
from jax.experimental.pallas import tpu as pltpu

F32 = jnp.float32
BF16 = jnp.bfloat16
EPS = 1e-6
ROWS = 256
HALO = 16
LANES = 128
V7X_VMEM_BYTES = 64 * 1024 * 1024
VMEM_CAP = V7X_VMEM_BYTES - 6 * 1024 * 1024
N_MOD = 6
GRID_W = 64
SSD_CHUNK = 128
SSD_STATE = 128
SSD_HEAD_DIM = 64
SSD_GROUPS = 8
GM_CHUNK = 128
TOP_K = 2


def _vmem_limit(nbytes):
    return int(min(VMEM_CAP, max(32 * 1024 * 1024, nbytes * 5 // 4 + (4 << 20))))


def _sigmoid(x):
    return 1.0 / (1.0 + jnp.exp(-x))


def _silu(x):
    return x * _sigmoid(x)


def _gelu_tanh(x):
    return 0.5 * x * (1.0 + jnp.tanh(math.sqrt(2.0 / math.pi) * (x + 0.044715 * (x * x * x))))


def _softplus(x):
    return jnp.maximum(x, 0.0) + jnp.log1p(jnp.exp(-jnp.abs(x)))


class _Layout:
    def __init__(self, batch, seq, dec_batch, dec_seq):
        assert seq == ROWS and dec_seq % ROWS == 0
        self.ncb = batch
        self.cps = dec_seq // ROWS
        self.nb = batch + dec_batch * self.cps
        self.dec_batch = dec_batch
        self.tokens = self.nb * ROWS

    def seq_pos(self, blk):
        is_ctx = blk < self.ncb
        pos = jnp.where(is_ctx, 0, (blk - self.ncb) % self.cps)
        ln = jnp.where(is_ctx, 1, self.cps)
        return pos, ln, is_ctx

    def cond_row(self, blk):
        return jnp.where(blk < self.ncb, 0, 1 + (blk - self.ncb) // self.cps)


def _mm_body(*refs, n_w, act, nk, k_rem, x_pro, has_bias, has_prev, has_scale):
    it = iter(refs)
    x_ref = next(it)
    w_refs = [next(it) for _ in range(n_w)]
    bias_ref = next(it) if has_bias else None
    prev_ref = next(it) if has_prev else None
    scale_ref = next(it) if has_scale else None
    o_ref = next(it)
    acc_refs = [next(it) for _ in range(n_w)] if nk > 1 else None

    def partial(masked):
        xv = x_ref[...]
        if x_pro == "silu":
            xv = _silu(xv.astype(F32)).astype(BF16)
        outs = []
        for w_ref in w_refs:
            wv = w_ref[...]
            xx = xv
            if masked:
                wv = jnp.where(lax.broadcasted_iota(jnp.int32, wv.shape, 0) < k_rem, wv, 0.0)
                xx = jnp.where(lax.broadcasted_iota(jnp.int32, xx.shape, 1) < k_rem, xx, jnp.zeros_like(xx))
            outs.append(jnp.dot(xx, wv.astype(BF16), preferred_element_type=F32))
        return outs

    def epilogue(accs):
        a = accs[0]
        if has_bias:
            a = a + bias_ref[...]
        if act == "gelu":
            a = _gelu_tanh(a)
        elif act == "swiglu":
            a = _silu(a) * accs[1]
        elif act == "glu":
            a = a * _sigmoid(accs[1])
        if has_scale:
            a = a * scale_ref[...]
        if has_prev:
            a = a + prev_ref[...]
        o_ref[...] = a.astype(o_ref.dtype)

    if nk == 1:
        epilogue(partial(False))
        return

    k = pl.program_id(2)

    def accumulate(masked):
        outs = partial(masked)

        @pl.when(k == 0)
        def _():
            for acc, o in zip(acc_refs, outs):
                acc[...] = o

        @pl.when(k > 0)
        def _():
            for acc, o in zip(acc_refs, outs):
                acc[...] += o

    if k_rem:
        pl.when(k < nk - 1)(lambda: accumulate(False))
        pl.when(k == nk - 1)(lambda: accumulate(True))
    else:
        accumulate(False)

    @pl.when(k == nk - 1)
    def _():
        epilogue([acc[...] for acc in acc_refs])


def _mm(x, ws, n_out, *, tm, tn, tk, act="none", out_dtype=BF16, bias=None, prev=None,
        scale=None, x_pro=None, name="mm"):
    m, kdim = x.shape
    tk = min(tk, kdim)
    nk = pl.cdiv(kdim, tk)
    k_rem = kdim % tk
    n_w = len(ws)
    assert m % tm == 0 and n_out % tn == 0
    in_specs = [pl.BlockSpec((tm, tk), lambda i, j, k: (i, k))]
    args = [x]
    for w, lead, col0 in ws:
        assert col0 % tn == 0
        in_specs.append(pl.BlockSpec((None,) * len(lead) + (tk, tn),
                                     functools.partial(lambda i, j, k, lead, cb: lead + (k, j + cb),
                                                       lead=tuple(lead), cb=col0 // tn)))
        args.append(w)
    if bias is not None:
        in_specs.append(pl.BlockSpec((1, tn), lambda i, j, k: (0, j)))
        args.append(bias)
    aliases = {}
    if prev is not None:
        aliases = {len(args): 0}
        in_specs.append(pl.BlockSpec((tm, tn), lambda i, j, k: (i, j)))
        args.append(prev)
    if scale is not None:
        in_specs.append(pl.BlockSpec((tm, 1), lambda i, j, k: (i, 0)))
        args.append(scale)
    scratch = [pltpu.VMEM((tm, tn), F32) for _ in range(n_w)] if nk > 1 else []
    osize = jnp.dtype(out_dtype).itemsize
    est = (2 * tm * tk * x.dtype.itemsize + n_w * (2 * tk * tn * 4 + tk * tn * 2 + 2 * tm * tn * 4)
           + 2 * tm * tn * osize + (2 * tm * tn * 4 if prev is not None else 0))
    body = functools.partial(_mm_body, n_w=n_w, act=act, nk=nk, k_rem=k_rem, x_pro=x_pro,
                             has_bias=bias is not None, has_prev=prev is not None,
                             has_scale=scale is not None)
    return pl.pallas_call(
        body,
        grid=(m // tm, n_out // tn, nk),
        in_specs=in_specs,
        out_specs=pl.BlockSpec((tm, tn), lambda i, j, k: (i, j)),
        out_shape=jax.ShapeDtypeStruct((m, n_out), out_dtype),
        scratch_shapes=scratch,
        input_output_aliases=aliases,
        compiler_params=pltpu.CompilerParams(
            dimension_semantics=("parallel", "parallel", "arbitrary"),
            vmem_limit_bytes=_vmem_limit(est)),
        name=name,
    )(*args)


def _rms(x):
    return x * lax.rsqrt(jnp.mean(x * x, axis=-1, keepdims=True) + EPS)


def _mod_spec(lay, layer, which, d):
    return pl.BlockSpec((None, None, 1, d), lambda i: (layer, lay.cond_row(i), 0, which))


def _vec_spec(layer, d):
    return pl.BlockSpec((None, 1, d), lambda i: (layer, 0, 0))


def _row_spec(d):
    return pl.BlockSpec((ROWS, d), lambda i: (i, 0))


def _entry_body(x_ref, pos_ref, g_ref, sc_ref, sh_ref, xo_ref, h_ref, *, lay):
    i = pl.program_id(0)
    x = x_ref[...] + jnp.where(i >= lay.ncb, pos_ref[...], 0.0)
    xo_ref[...] = x
    h_ref[...] = (_rms(x) * g_ref[...] * (1.0 + sc_ref[...]) + sh_ref[...]).astype(BF16)


def _entry(lay, x, pos, g_pre, mod):
    d = x.shape[1]
    cps = lay.cps
    return pl.pallas_call(
        functools.partial(_entry_body, lay=lay),
        grid=(lay.nb,),
        in_specs=[_row_spec(d),
                  pl.BlockSpec((ROWS, d), lambda i: (jnp.maximum(i - lay.ncb, 0) % cps, 0)),
                  _vec_spec(0, d), _mod_spec(lay, 0, 1, d), _mod_spec(lay, 0, 0, d)],
        out_specs=[_row_spec(d), _row_spec(d)],
        out_shape=[jax.ShapeDtypeStruct(x.shape, F32), jax.ShapeDtypeStruct(x.shape, BF16)],
        compiler_params=pltpu.CompilerParams(dimension_semantics=("parallel",),
                                             vmem_limit_bytes=_vmem_limit(10 * ROWS * d * 4)),
        name="entry",
    )(x, pos, g_pre, mod, mod)


def _post_body(*refs, pre, route, n_exp):
    it = iter(refs)
    x_ref, y_ref, gpost_ref, gate_ref = next(it), next(it), next(it), next(it)
    if pre:
        gpre_ref, sc_ref, sh_ref = next(it), next(it), next(it)
    if route:
        rw_ref = next(it)
    xo_ref = next(it)
    x = x_ref[...] + gate_ref[...] * (_rms(y_ref[...]) * gpost_ref[...])
    xo_ref[...] = x
    if not pre:
        return
    h_ref = next(it)
    h = _rms(x) * gpre_ref[...] * (1.0 + sc_ref[...]) + sh_ref[...]
    h_ref[...] = h.astype(BF16)
    if route:
        comb_ref = next(it)
        logits = jnp.dot(h, rw_ref[...], preferred_element_type=F32, precision=lax.Precision.HIGHEST)
        lane = lax.broadcasted_iota(jnp.int32, logits.shape, 1)
        neg = jnp.float32(-jnp.inf)
        l1 = jnp.where(lane < n_exp, logits, neg)
        m1 = jnp.max(l1, axis=-1, keepdims=True)
        i1 = jnp.min(jnp.where(l1 == m1, lane, LANES), axis=-1, keepdims=True)
        l2 = jnp.where(lane == i1, neg, l1)
        m2 = jnp.max(l2, axis=-1, keepdims=True)
        i2 = jnp.min(jnp.where(l2 == m2, lane, LANES), axis=-1, keepdims=True)
        e2 = jnp.exp(m2 - m1)
        g1 = 1.0 / (1.0 + e2)
        g2 = e2 / (1.0 + e2)
        comb_ref[...] = jnp.where(lane == i1, g1, 0.0) + jnp.where(lane == i2, g2, 0.0)


def _post(lay, x, y, g_post, mod, layer, gate_idx, pre=None, router=None, n_exp=0):
    d = x.shape[1]
    in_specs = [_row_spec(d), _row_spec(d), _vec_spec(layer, d), _mod_spec(lay, layer, gate_idx, d)]
    args = [x, y, g_post, mod]
    out_specs = [_row_spec(d)]
    out_shape = [jax.ShapeDtypeStruct(x.shape, F32)]
    if pre is not None:
        g_pre, l2, sc_i, sh_i = pre
        in_specs += [_vec_spec(l2, d), _mod_spec(lay, l2, sc_i, d), _mod_spec(lay, l2, sh_i, d)]
        args += [g_pre, mod, mod]
        out_specs.append(_row_spec(d))
        out_shape.append(jax.ShapeDtypeStruct(x.shape, BF16))
    if router is not None:
        in_specs.append(pl.BlockSpec((d, LANES), lambda i: (0, 0)))
        args.append(router)
        out_specs.append(_row_spec(LANES))
        out_shape.append(jax.ShapeDtypeStruct((x.shape[0], LANES), F32))
    outs = pl.pallas_call(
        functools.partial(_post_body, pre=pre is not None, route=router is not None, n_exp=n_exp),
        grid=(lay.nb,),
        in_specs=in_specs, out_specs=out_specs, out_shape=out_shape,
        compiler_params=pltpu.CompilerParams(dimension_semantics=("parallel",),
                                             vmem_limit_bytes=_vmem_limit(14 * ROWS * d * 4)),
        name="post",
    )(*args)
    return outs


def _conv_body(prev_ref, x_ref, next_ref, w_ref, b_ref, o_ref, xp_ref, *, lay, ktaps, act, rt):
    i = pl.program_id(0)
    pos, ln, _ = lay.seq_pos(i)
    xp_ref[0:HALO, :] = jnp.where(pos != 0, prev_ref[...].astype(F32), 0.0)
    xp_ref[HALO:HALO + ROWS, :] = x_ref[...].astype(F32)
    xp_ref[HALO + ROWS:HALO + ROWS + HALO, :] = jnp.where(pos != ln - 1, next_ref[...].astype(F32), 0.0)
    pad = ktaps // 2
    tc = o_ref.shape[1]
    for r in range(ROWS // rt):
        acc = jnp.broadcast_to(b_ref[...], (rt, tc))
        for k in range(ktaps):
            acc = acc + w_ref[pl.ds(k, 1), :] * xp_ref[pl.ds(HALO - pad + k + r * rt, rt), :]
        if act == "silu":
            acc = _silu(acc)
        o_ref[r * rt:(r + 1) * rt, :] = acc.astype(o_ref.dtype)


def _dwconv(lay, x, col0, width, w, b, layer, act, out_dtype, tc=512, rt=32):
    ktaps = w.shape[1]
    assert ktaps // 2 <= HALO and col0 % tc == 0 and width % tc == 0
    cb = col0 // tc
    hb = ROWS // HALO
    last = lay.tokens // HALO - 1
    return pl.pallas_call(
        functools.partial(_conv_body, lay=lay, ktaps=ktaps, act=act, rt=rt),
        grid=(lay.nb, width // tc),
        in_specs=[pl.BlockSpec((HALO, tc), lambda i, c: (jnp.maximum(i * hb - 1, 0), c + cb)),
                  pl.BlockSpec((ROWS, tc), lambda i, c: (i, c + cb)),
                  pl.BlockSpec((HALO, tc), lambda i, c: (jnp.minimum((i + 1) * hb, last), c + cb)),
                  pl.BlockSpec((None, ktaps, tc), lambda i, c: (layer, 0, c)),
                  pl.BlockSpec((None, 1, tc), lambda i, c: (layer, 0, c))],
        out_specs=pl.BlockSpec((ROWS, tc), lambda i, c: (i, c)),
        out_shape=jax.ShapeDtypeStruct((lay.tokens, width), out_dtype),
        scratch_shapes=[pltpu.VMEM((ROWS + 2 * HALO, tc), F32)],
        compiler_params=pltpu.CompilerParams(dimension_semantics=("parallel", "parallel")),
        name="dwconv%d" % ktaps,
    )(x, x, x, w, b)


def _dt_body(x_ref, w_ref, b_ref, dt_ref, dtt_ref):
    a = jnp.dot(x_ref[...], w_ref[...].astype(BF16), preferred_element_type=F32) + b_ref[...]
    dt = _softplus(a)
    dt_ref[...] = dt
    dtt_ref[...] = dt.T


def _dt_proj(h, w_in, layer, col0, bias, tm=512):
    m, kdim = h.shape
    return pl.pallas_call(
        _dt_body,
        grid=(m // tm,),
        in_specs=[pl.BlockSpec((tm, kdim), lambda i: (i, 0)),
                  pl.BlockSpec((None, kdim, LANES), lambda i: (layer, 0, col0 // LANES)),
                  pl.BlockSpec((1, LANES), lambda i: (0, 0))],
        out_specs=[pl.BlockSpec((tm, LANES), lambda i: (i, 0)),
                   pl.BlockSpec((LANES, tm), lambda i: (0, i))],
        out_shape=[jax.ShapeDtypeStruct((m, LANES), F32), jax.ShapeDtypeStruct((LANES, m), F32)],
        compiler_params=pltpu.CompilerParams(dimension_semantics=("parallel",)),
        name="dt_proj",
    )(h, w_in, bias)


def _split3(x):
    hi = x.astype(BF16)
    r1 = x - hi.astype(F32)
    mid = r1.astype(BF16)
    lo = (r1 - mid.astype(F32)).astype(BF16)
    return hi, mid, lo


_NT = (((1,), (1,)), ((), ()))
_TN = (((0,), (0,)), ((), ()))


def _ssd_body(xs_ref, b_ref, c_ref, dt_ref, dtt_ref, arow_ref, acol_ref, h0_ref, y_ref, st_ref, h_scr,
              *, lay, heads_per_group):
    t = SSD_CHUNK
    p = SSD_HEAD_DIM
    d = pl.program_id(0)
    g = pl.program_id(1)
    s = pl.program_id(2)
    blk = jnp.where(d == 0, s, lay.nb - 1 - s)
    pos, ln, is_ctx = lay.seq_pos(blk)
    first = jnp.where(d == 0, pos == 0, pos == ln - 1)
    last = jnp.where(d == 0, pos == ln - 1, pos == 0)

    @pl.when(jnp.logical_and(first, is_ctx))
    def _():
        h_scr[...] = jnp.zeros_like(h_scr)

    @pl.when(jnp.logical_and(first, jnp.logical_not(is_ctx)))
    def _():
        h_scr[...] = h0_ref[...]

    shift = (LANES - (d * (LANES // 2) + g * heads_per_group)) % LANES
    a_c = pltpu.roll(jnp.broadcast_to(arow_ref[...], (8, LANES)), shift, 1)[0:1, :]
    row = lax.broadcasted_iota(jnp.int32, (t, t), 0)
    col = lax.broadcasted_iota(jnp.int32, (t, t), 1)
    mask = (row - col) * (1 - 2 * d) >= 0
    maskb = mask.astype(BF16)
    lane = lax.broadcasted_iota(jnp.int32, (t, LANES), 1)
    low = lane < p

    for j in range(ROWS // t):
        cj = jnp.where(d == 0, j, ROWS // t - 1 - j)
        r0 = pl.multiple_of(cj * t, t)
        x = xs_ref[pl.ds(r0, t), :]
        bm = b_ref[pl.ds(r0, t), :]
        cm = c_ref[pl.ds(r0, t), :]
        dtc = pltpu.roll(dt_ref[pl.ds(r0, t), :], shift, 1)
        dac = dtc * a_c
        dtr = dtt_ref[:, pl.ds(r0, t)]
        dar = dtr * acol_ref[...]
        cs_col = sum(jnp.dot(maskb, part, preferred_element_type=F32) for part in _split3(dac))
        cs_row = sum(lax.dot_general(part, maskb, _NT, preferred_element_type=F32) for part in _split3(dar))
        tot_l = jnp.where(d == 0, cs_col[t - 1:t, :], cs_col[0:1, :])
        tot_r = jnp.sum(dar, axis=1, keepdims=True)
        cb = lax.dot_general(cm, bm, _NT, preferred_element_type=F32)
        e_in = jnp.exp(cs_col)
        w_st = dtc * jnp.exp(tot_l - cs_col)
        ch = lax.dot_general(cm, h_scr[...].astype(BF16), _NT, preferred_element_type=F32)
        xw_parts = []
        for pr in range(heads_per_group // 2):
            xpair = x[:, pr * LANES:(pr + 1) * LANES]
            ypair = None
            for r in (2 * pr, 2 * pr + 1):
                seg = jnp.exp(jnp.where(mask, cs_col[:, r:r + 1] - cs_row[r:r + 1, :], -jnp.inf))
                mh = (cb * seg * dtr[r:r + 1, :]).astype(BF16)
                xh = jnp.where(low if r % 2 == 0 else jnp.logical_not(low), xpair, jnp.zeros_like(xpair))
                yd = jnp.dot(mh, xh, preferred_element_type=F32)
                ypair = yd if ypair is None else ypair + yd
            esc = jnp.where(low, e_in[:, 2 * pr:2 * pr + 1], e_in[:, 2 * pr + 1:2 * pr + 2])
            wsc = jnp.where(low, w_st[:, 2 * pr:2 * pr + 1], w_st[:, 2 * pr + 1:2 * pr + 2])
            y_ref[pl.ds(r0, t), pr * LANES:(pr + 1) * LANES] = ypair + ch[:, pr * LANES:(pr + 1) * LANES] * esc
            xw_parts.append((xpair.astype(F32) * wsc).astype(BF16))
        xw = jnp.concatenate(xw_parts, axis=1)
        st = lax.dot_general(xw, bm, _TN, preferred_element_type=F32)
        dec = jnp.exp(tot_r)
        for r in range(heads_per_group):
            h_scr[r * p:(r + 1) * p, :] = h_scr[r * p:(r + 1) * p, :] * dec[r:r + 1, :] + st[r * p:(r + 1) * p, :]

    @pl.when(jnp.logical_and(last, is_ctx))
    def _():
        st_ref[...] = h_scr[...]


def _ssd(lay, xa, dt, dtt, a_row, a_col, h0, layer, inner):
    r = inner // SSD_HEAD_DIM // SSD_GROUPS
    assert r == 8 and 2 * SSD_GROUPS * r == LANES
    gw = r * SSD_HEAD_DIM
    n = SSD_STATE
    nb = lay.nb

    def blk(d, s):
        return jnp.where(d == 0, s, nb - 1 - s)

    def seq_s(d, s):
        return jnp.clip((blk(d, s) - lay.ncb) // lay.cps, 0, lay.dec_batch - 1)

    return pl.pallas_call(
        functools.partial(_ssd_body, lay=lay, heads_per_group=r),
        grid=(2, SSD_GROUPS, nb),
        in_specs=[pl.BlockSpec((ROWS, gw), lambda d, g, s: (blk(d, s), g)),
                  pl.BlockSpec((ROWS, n), lambda d, g, s: (blk(d, s), inner // n + g)),
                  pl.BlockSpec((ROWS, n), lambda d, g, s: (blk(d, s), inner // n + SSD_GROUPS + g)),
                  pl.BlockSpec((ROWS, LANES), lambda d, g, s: (blk(d, s), 0)),
                  pl.BlockSpec((r, ROWS), lambda d, g, s: (d * SSD_GROUPS + g, blk(d, s))),
                  pl.BlockSpec((1, LANES), lambda d, g, s: (0, 0)),
                  pl.BlockSpec((r, LANES), lambda d, g, s: (d * SSD_GROUPS + g, 0)),
                  pl.BlockSpec((None, None, None, gw, n), lambda d, g, s: (seq_s(d, s), layer, d, g, 0))],
        out_specs=[pl.BlockSpec((None, ROWS, gw), lambda d, g, s: (d, blk(d, s), g)),
                   pl.BlockSpec((None, None, gw, n),
                                lambda d, g, s: (jnp.minimum(blk(d, s), lay.ncb - 1), d, g, 0))],
        out_shape=[jax.ShapeDtypeStruct((2, lay.tokens, inner), F32),
                   jax.ShapeDtypeStruct((lay.ncb, 2, inner, n), F32)],
        scratch_shapes=[pltpu.VMEM((gw, n), F32)],
        compiler_params=pltpu.CompilerParams(dimension_semantics=("arbitrary", "arbitrary", "arbitrary")),
        name="ssd_scan",
    )(xa, xa, xa, dt, dtt, a_row, a_col, h0)


def _gate_body(yf_ref, yb_ref, xs_ref, z_ref, dsk_ref, g_ref, o_ref):
    y = yf_ref[...] + yb_ref[...] + xs_ref[...].astype(F32) * dsk_ref[...]
    y = y * _silu(z_ref[...].astype(F32))
    o_ref[...] = (_rms(y) * g_ref[...]).astype(o_ref.dtype)


def _ssd_gate(lay, y2, xa, zx, dskip, norm, layer, inner, out_width):
    return pl.pallas_call(
        _gate_body,
        grid=(lay.nb,),
        in_specs=[pl.BlockSpec((None, ROWS, inner), lambda i: (0, i, 0)),
                  pl.BlockSpec((None, ROWS, inner), lambda i: (1, i, 0)),
                  _row_spec(inner), _row_spec(inner), _vec_spec(layer, inner), _vec_spec(layer, inner)],
        out_specs=_row_spec(inner),
        out_shape=jax.ShapeDtypeStruct((lay.tokens, out_width), BF16),
        compiler_params=pltpu.CompilerParams(dimension_semantics=("parallel",),
                                             vmem_limit_bytes=_vmem_limit(12 * ROWS * inner * 4)),
        name="ssd_gate",
    )(y2, y2, xa, zx, dskip, norm)


def _ln_silu_body(x_ref, g_ref, b_ref, yu_ref, o_ref):
    del yu_ref
    x = x_ref[...]
    mu = jnp.mean(x, axis=-1, keepdims=True)
    xc = x - mu
    var = jnp.mean(xc * xc, axis=-1, keepdims=True)
    o_ref[...] = _silu(xc * lax.rsqrt(var + EPS) * g_ref[...] + b_ref[...]).astype(o_ref.dtype)


def _ln_silu_into(lay, x, g, b, layer, yu, col_block):
    d = x.shape[1]
    return pl.pallas_call(
        _ln_silu_body,
        grid=(lay.nb,),
        in_specs=[_row_spec(d), _vec_spec(layer, d), _vec_spec(layer, d),
                  pl.BlockSpec(memory_space=pl.ANY)],
        out_specs=pl.BlockSpec((ROWS, d), lambda i: (i, col_block)),
        out_shape=jax.ShapeDtypeStruct(yu.shape, yu.dtype),
        input_output_aliases={3: 0},
        compiler_params=pltpu.CompilerParams(dimension_semantics=("parallel",),
                                             vmem_limit_bytes=_vmem_limit(8 * ROWS * d * 4)),
        name="cfm_ln",
    )(x, g, b, yu)


def _gmlp_body(u_ref, v_ref, g_ref, b_ref, ws_ref, bs_ref, o_ref, vn_ref, *, groups):
    v = v_ref[...].astype(F32)
    mu = jnp.mean(v, axis=-1, keepdims=True)
    vc = v - mu
    var = jnp.mean(vc * vc, axis=-1, keepdims=True)
    vn_ref[...] = (vc * lax.rsqrt(var + EPS) * g_ref[...] + b_ref[...]).astype(BF16)
    t = GM_CHUNK
    gk = v.shape[1] // groups
    for gi in range(groups):
        wg = ws_ref[gi].astype(BF16)
        bias = bs_ref[:, gi:gi + 1]
        for c in range(ROWS // t):
            rs = slice(c * t, (c + 1) * t)
            cs = slice(gi * gk, (gi + 1) * gk)
            sv = jnp.dot(wg, vn_ref[rs, cs], preferred_element_type=F32) + bias
            o_ref[rs, cs] = (u_ref[rs, cs].astype(F32) * sv).astype(o_ref.dtype)


def _gmlp_gate(lay, uv, ln_g, ln_b, w_s, b_st, layer):
    dim = uv.shape[1] // 2
    groups = w_s.shape[1]
    return pl.pallas_call(
        functools.partial(_gmlp_body, groups=groups),
        grid=(lay.nb,),
        in_specs=[pl.BlockSpec((ROWS, dim), lambda i: (i, 0)),
                  pl.BlockSpec((ROWS, dim), lambda i: (i, 1)),
                  _vec_spec(layer, dim), _vec_spec(layer, dim),
                  pl.BlockSpec((None, groups, GM_CHUNK, GM_CHUNK), lambda i: (layer, 0, 0, 0)),
                  pl.BlockSpec((None, GM_CHUNK, groups), lambda i: (layer, 0, 0))],
        out_specs=_row_spec(dim),
        out_shape=jax.ShapeDtypeStruct((lay.tokens, dim), BF16),
        scratch_shapes=[pltpu.VMEM((ROWS, dim), BF16)],
        compiler_params=pltpu.CompilerParams(dimension_semantics=("parallel",),
                                             vmem_limit_bytes=_vmem_limit(12 * ROWS * dim * 4)),
        name="gmlp_gate",
    )(uv, uv, ln_g, ln_b, w_s, b_st)


def _grid_pos_embed(n_tokens, dim):
    rows = n_tokens // GRID_W
    quarter = dim // 4
    omega = 1.0 / (10000.0 ** (jnp.arange(quarter, dtype=F32) / quarter))

    def axis_embed(n):
        ang = jnp.arange(n, dtype=F32)[:, None] * omega[None, :]
        return jnp.concatenate([jnp.sin(ang), jnp.cos(ang)], axis=-1)

    er = jnp.broadcast_to(axis_embed(rows)[:, None, :], (rows, GRID_W, dim // 2))
    ec = jnp.broadcast_to(axis_embed(GRID_W)[None, :, :], (rows, GRID_W, dim // 2))
    return jnp.concatenate([er, ec], axis=-1).reshape(n_tokens, dim)


def _v3(a):
    return a.reshape(a.shape[0], 1, a.shape[1])


def kernel(x_prompt, x_sample, state_ssd, c, c_ctx, w_mod, b_mod, norm_mix_pre, norm_mix_post, norm_ffn_pre, norm_ffn_post, w_in_even, ssd_conv_w, ssd_conv_b, ssd_dt_bias, ssd_a_log, ssd_d, ssd_norm, cfm_conv_w, cfm_conv_b, cfm_ln_g, cfm_ln_b, w_out_even, w_in_odd, gm_ln_g, gm_ln_b, gm_w_s, gm_b_s, w_out_odd, ffn_w1, ffn_w3, ffn_w2, router_w, moe_w1, moe_w3, moe_w2):
    batch, seq, d = x_prompt.shape
    dec_batch, dec_seq, _ = x_sample.shape
    depth = w_mod.shape[0]
    lay = _Layout(batch, seq, dec_batch, dec_seq)
    tokens = lay.tokens
    inner = d
    heads = inner // SSD_HEAD_DIM
    bc_dim = SSD_GROUPS * SSD_STATE
    conv_dim = inner + 2 * bc_dim
    cfm_dim = cfm_conv_w.shape[2]
    dt_col = inner + conv_dim
    glu_col = dt_col + 2 * heads
    n_exp = router_w.shape[2]
    d_ff = ffn_w1.shape[2]
    d_ffe = moe_w1.shape[3]
    tm = 1024 if tokens % 1024 == 0 else 512

    n_cond = 1 + dec_batch
    cond = jnp.zeros((8, d), F32).at[0].set(c_ctx).at[1:n_cond].set(c)
    mod = jnp.stack([
        _mm(cond, [(w_mod, (l,), 0)], N_MOD * d, tm=8, tn=512, tk=d, out_dtype=F32,
            bias=b_mod[l][None, :], x_pro="silu", name="adaln_mod")
        for l in range(depth)])
    mod = mod.reshape(depth, 8, 1, N_MOD * d)

    x = jnp.concatenate([x_prompt.reshape(batch * seq, d), x_sample.reshape(dec_batch * dec_seq, d)], axis=0)
    pos = _grid_pos_embed(dec_seq, d)
    g_mix_pre, g_mix_post = _v3(norm_mix_pre), _v3(norm_mix_post)
    g_ffn_pre, g_ffn_post = _v3(norm_ffn_pre), _v3(norm_ffn_post)
    x, h = _entry(lay, x, pos, g_mix_pre, mod)

    a_neg = -jnp.exp(ssd_a_log.astype(F32))
    h0_all = state_ssd.reshape(dec_batch, state_ssd.shape[1], 2, inner, SSD_STATE)
    dskip = _v3(jnp.repeat(ssd_d, SSD_HEAD_DIM, axis=1))
    glu_w = w_in_even[:, :, glu_col:]
    states = []
    for l in range(depth):
        i = l // 2
        if l % 2 == 0:
            zx = _mm(h, [(w_in_even, (i,), 0)], dt_col, tm=tm, tn=512, tk=d, name="even_in_zx")
            dt, dtt = _dt_proj(h, w_in_even, i, dt_col, ssd_dt_bias[i].reshape(1, 2 * heads))
            u0 = _mm(h, [(glu_w, (i,), 0), (glu_w, (i,), cfm_dim)], cfm_dim, tm=tm, tn=256, tk=d,
                     act="glu", name="even_in_glu")
            xa = _dwconv(lay, zx, inner, conv_dim, ssd_conv_w, _v3(ssd_conv_b), i, "silu", BF16)
            a_row = a_neg[i].reshape(1, 2 * heads)
            a_col = jnp.broadcast_to(a_neg[i].reshape(2 * heads, 1), (2 * heads, LANES))
            y2, st = _ssd(lay, xa, dt, dtt, a_row, a_col, h0_all, i, inner)
            states.append(st)
            yu = _ssd_gate(lay, y2, xa, zx, dskip, _v3(ssd_norm), i, inner, inner + cfm_dim)
            uc = _dwconv(lay, u0, 0, cfm_dim, cfm_conv_w, _v3(cfm_conv_b), i, "none", F32)
            yu = _ln_silu_into(lay, uc, _v3(cfm_ln_g), _v3(cfm_ln_b), i, yu, inner // cfm_dim)
            y = _mm(yu, [(w_out_even, (i,), 0)], d, tm=tm, tn=1024, tk=1024, out_dtype=F32, name="even_out")
        else:
            uv = _mm(h, [(w_in_odd, (i,), 0)], w_in_odd.shape[2], tm=tm, tn=512, tk=d, act="gelu",
                     name="odd_in")
            gated = _gmlp_gate(lay, uv, _v3(gm_ln_g), _v3(gm_ln_b), gm_w_s,
                               jnp.swapaxes(gm_b_s, 1, 2), i)
            y = _mm(gated, [(w_out_odd, (i,), 0)], d, tm=tm, tn=1024, tk=1024, out_dtype=F32, name="odd_out")

        if l % 2 == 0:
            x, h = _post(lay, x, y, g_mix_post, mod, l, 2, pre=(g_ffn_pre, l, 4, 3))
            act = _mm(h, [(ffn_w1, (i,), 0), (ffn_w3, (i,), 0)], d_ff, tm=tm, tn=256, tk=d, act="swiglu",
                      name="ffn_up")
            y = _mm(act, [(ffn_w2, (i,), 0)], d, tm=tm, tn=1024, tk=1024, out_dtype=F32, name="ffn_down")
        else:
            rw = jnp.zeros((d, LANES), F32).at[:, :n_exp].set(router_w[i])
            x, h, comb = _post(lay, x, y, g_mix_post, mod, l, 2, pre=(g_ffn_pre, l, 4, 3), router=rw,
                               n_exp=n_exp)
            y = None
            for e in range(n_exp):
                act = _mm(h, [(moe_w1, (i, e), 0), (moe_w3, (i, e), 0)], d_ffe, tm=tm, tn=256, tk=d,
                          act="swiglu", name="moe_up")
                y = _mm(act, [(moe_w2, (i, e), 0)], d, tm=tm, tn=1024, tk=1024, out_dtype=F32,
                        prev=y, scale=comb[:, e:e + 1], name="moe_down")

        if l + 1 < depth:
            x, h = _post(lay, x, y, g_ffn_post, mod, l, 5, pre=(g_mix_pre, l + 1, 1, 0))
        else:
            (x,) = _post(lay, x, y, g_ffn_post, mod, l, 5)

    n_ctx = batch * seq
    y_prompt = x[:n_ctx].reshape(batch, seq, d)
    y_sample = x[n_ctx:].reshape(dec_batch, dec_seq, d)
    state_new = jnp.stack(states, axis=1).reshape(batch, len(states), 2, heads, SSD_HEAD_DIM, SSD_STATE)
    return (y_prompt, y_sample, state_new.astype(state_ssd.dtype))
```

```python
import functools
import math

import jax
import jax.numpy as jnp
from jax import lax
from jax.experimental import pallas as pl
from jax.experimental.pallas import tpu as pltpu

F32 = jnp.float32
BF16 = jnp.bfloat16
EPS = 1e-6
ROWS = 256
HALO = 16
LANES = 128
SUBLANES = 8
MOE_ROWS = 256
V7X_VMEM_BYTES = 64 * 1024 * 1024
VMEM_CAP = V7X_VMEM_BYTES - 6 * 1024 * 1024
N_MOD = 6
GRID_W = 64
SSD_CHUNK = 128
SSD_STATE = 128
SSD_HEAD_DIM = 64
SSD_GROUPS = 8
GM_CHUNK = 128
TOP_K = 2


def _vmem_limit(nbytes):
    return int(min(VMEM_CAP, max(32 * 1024 * 1024, nbytes * 5 // 4 + (4 << 20))))


def _sigmoid(x):
    return 1.0 / (1.0 + jnp.exp(-x))


def _silu(x):
    return x * _sigmoid(x)


def _gelu_tanh(x):
    return 0.5 * x * (1.0 + jnp.tanh(math.sqrt(2.0 / math.pi) * (x + 0.044715 * (x * x * x))))


def _softplus(x):
    return jnp.maximum(x, 0.0) + jnp.log1p(jnp.exp(-jnp.abs(x)))


class _Layout:
    def __init__(self, batch, seq, dec_batch, dec_seq):
        assert seq == ROWS and dec_seq % ROWS == 0
        self.ncb = batch
        self.cps = dec_seq // ROWS
        self.nb = batch + dec_batch * self.cps
        self.dec_batch = dec_batch
        self.tokens = self.nb * ROWS

    def seq_pos(self, blk):
        is_ctx = blk < self.ncb
        pos = jnp.where(is_ctx, 0, (blk - self.ncb) % self.cps)
        ln = jnp.where(is_ctx, 1, self.cps)
        return pos, ln, is_ctx

    def cond_row(self, blk):
        return jnp.where(blk < self.ncb, 0, 1 + (blk - self.ncb) // self.cps)


def _mm_body(*refs, n_x, n_w, act, x_pro, has_bias):
    it = iter(refs)
    x_refs = [next(it) for _ in range(n_x)]
    w_refs = [next(it) for _ in range(n_w)]
    bias_ref = next(it) if has_bias else None
    o_ref = next(it)
    accs = []
    for w_ref in w_refs:
        acc = None
        k0 = 0
        for x_ref in x_refs:
            xv = x_ref[...]
            if x_pro == "silu":
                xv = _silu(xv.astype(F32)).astype(BF16)
            kp = xv.shape[1]
            part = jnp.dot(xv, w_ref[k0:k0 + kp, :].astype(BF16), preferred_element_type=F32)
            acc = part if acc is None else acc + part
            k0 += kp
        accs.append(acc)
    a = accs[0]
    if has_bias:
        a = a + bias_ref[...]
    if act == "gelu":
        a = _gelu_tanh(a)
    elif act == "swiglu":
        a = _silu(a) * accs[1]
    elif act == "glu":
        a = a * _sigmoid(accs[1])
    o_ref[...] = a.astype(o_ref.dtype)


def _mm(xs, ws, n_out, *, tm, tn, act="none", out_dtype=BF16, bias=None, x_pro=None, x_buffers=2,
        name="mm"):
    m = xs[0].shape[0]
    kdim = sum(x.shape[1] for x in xs)
    n_w = len(ws)
    assert m % tm == 0 and n_out % tn == 0
    in_specs = []
    for x in xs:
        if x_buffers == 1:
            in_specs.append(pl.BlockSpec((tm, x.shape[1]), lambda i, j: (i, 0), pipeline_mode=pl.Buffered(1)))
        else:
            in_specs.append(pl.BlockSpec((tm, x.shape[1]), lambda i, j: (i, 0)))
    args = list(xs)
    for w, lead, col0 in ws:
        assert col0 % tn == 0 and w.shape[-2] == kdim
        in_specs.append(pl.BlockSpec((None,) * len(lead) + (kdim, tn),
                                     functools.partial(lambda i, j, lead, cb: lead + (0, j + cb),
                                                       lead=tuple(lead), cb=col0 // tn)))
        args.append(w)
    if bias is not None:
        in_specs.append(pl.BlockSpec((1, tn), lambda i, j: (0, j)))
        args.append(bias)
    osize = jnp.dtype(out_dtype).itemsize
    est = (x_buffers * tm * kdim * xs[0].dtype.itemsize + n_w * (2 * kdim * tn * 4 + kdim * tn * 2 + 2 * tm * tn * 4)
           + 2 * tm * tn * osize)
    body = functools.partial(_mm_body, n_x=len(xs), n_w=n_w, act=act, x_pro=x_pro, has_bias=bias is not None)
    return pl.pallas_call(
        body,
        grid=(m // tm, n_out // tn),
        in_specs=in_specs,
        out_specs=pl.BlockSpec((tm, tn), lambda i, j: (i, j)),
        out_shape=jax.ShapeDtypeStruct((m, n_out), out_dtype),
        compiler_params=pltpu.CompilerParams(
            dimension_semantics=("parallel", "parallel"),
            vmem_limit_bytes=_vmem_limit(est)),
        name=name,
    )(*args)


def _rms(x):
    return x * lax.rsqrt(jnp.mean(x * x, axis=-1, keepdims=True) + EPS)


def _mod_spec(lay, layer, which, d):
    return pl.BlockSpec((None, None, 1, d), lambda i: (layer, lay.cond_row(i), 0, which))


def _vec_spec(layer, d):
    return pl.BlockSpec((None, 1, d), lambda i: (layer, 0, 0))


def _row_spec(d):
    return pl.BlockSpec((ROWS, d), lambda i: (i, 0))


def _entry_body(x_ref, pos_ref, g_ref, sc_ref, sh_ref, xo_ref, h_ref, *, lay):
    i = pl.program_id(0)
    x = x_ref[...] + jnp.where(i >= lay.ncb, pos_ref[...], 0.0)
    xo_ref[...] = x
    h_ref[...] = (_rms(x) * g_ref[...] * (1.0 + sc_ref[...]) + sh_ref[...]).astype(BF16)


def _entry(lay, x, pos, g_pre, mod):
    d = x.shape[1]
    cps = lay.cps
    return pl.pallas_call(
        functools.partial(_entry_body, lay=lay),
        grid=(lay.nb,),
        in_specs=[_row_spec(d),
                  pl.BlockSpec((ROWS, d), lambda i: (jnp.maximum(i - lay.ncb, 0) % cps, 0)),
                  _vec_spec(0, d), _mod_spec(lay, 0, 1, d), _mod_spec(lay, 0, 0, d)],
        out_specs=[_row_spec(d), _row_spec(d)],
        out_shape=[jax.ShapeDtypeStruct(x.shape, F32), jax.ShapeDtypeStruct(x.shape, BF16)],
        compiler_params=pltpu.CompilerParams(dimension_semantics=("parallel",),
                                             vmem_limit_bytes=_vmem_limit(10 * ROWS * d * 4)),
        name="entry",
    )(x, pos, g_pre, mod, mod)


def _post_body(*refs, pre, route, n_exp):
    it = iter(refs)
    x_ref, y_ref, gpost_ref, gate_ref = next(it), next(it), next(it), next(it)
    if pre:
        gpre_ref, sc_ref, sh_ref = next(it), next(it), next(it)
    if route:
        rw_ref = next(it)
    xo_ref = next(it)
    x = x_ref[...] + gate_ref[...] * (_rms(y_ref[...]) * gpost_ref[...])
    xo_ref[...] = x
    if not pre:
        return
    h_ref = next(it)
    h = _rms(x) * gpre_ref[...] * (1.0 + sc_ref[...]) + sh_ref[...]
    h_ref[...] = h.astype(BF16)
    if route:
        hf_ref = next(it)
        route_ref = next(it)
        hf_ref[...] = h
        logits = jnp.dot(h, rw_ref[...], preferred_element_type=F32, precision=lax.Precision.HIGHEST)
        lane = lax.broadcasted_iota(jnp.int32, logits.shape, 1)
        neg = jnp.float32(-jnp.inf)
        l1 = jnp.where(lane < n_exp, logits, neg)
        m1 = jnp.max(l1, axis=-1, keepdims=True)
        i1 = jnp.min(jnp.where(l1 == m1, lane, LANES), axis=-1, keepdims=True)
        l2 = jnp.where(lane == i1, neg, l1)
        m2 = jnp.max(l2, axis=-1, keepdims=True)
        i2 = jnp.min(jnp.where(l2 == m2, lane, LANES), axis=-1, keepdims=True)
        e2 = jnp.exp(m2 - m1)
        g1 = 1.0 / (1.0 + e2)
        g2 = e2 / (1.0 + e2)
        route_ref[...] = jnp.where(lane == 0, i1.astype(F32),
                                   jnp.where(lane == 1, i2.astype(F32),
                                             jnp.where(lane == 2, g1, jnp.where(lane == 3, g2, 0.0))))


def _post(lay, x, y, g_post, mod, layer, gate_idx, pre=None, router=None, n_exp=0):
    d = x.shape[1]
    in_specs = [_row_spec(d), _row_spec(d), _vec_spec(layer, d), _mod_spec(lay, layer, gate_idx, d)]
    args = [x, y, g_post, mod]
    out_specs = [_row_spec(d)]
    out_shape = [jax.ShapeDtypeStruct(x.shape, F32)]
    if pre is not None:
        g_pre, l2, sc_i, sh_i = pre
        in_specs += [_vec_spec(l2, d), _mod_spec(lay, l2, sc_i, d), _mod_spec(lay, l2, sh_i, d)]
        args += [g_pre, mod, mod]
        out_specs.append(_row_spec(d))
        out_shape.append(jax.ShapeDtypeStruct(x.shape, BF16))
    if router is not None:
        in_specs.append(pl.BlockSpec((d, LANES), lambda i: (0, 0)))
        args.append(router)
        out_specs += [_row_spec(d), _row_spec(LANES)]
        out_shape += [jax.ShapeDtypeStruct(x.shape, F32), jax.ShapeDtypeStruct((x.shape[0], LANES), F32)]
    outs = pl.pallas_call(
        functools.partial(_post_body, pre=pre is not None, route=router is not None, n_exp=n_exp),
        grid=(lay.nb,),
        in_specs=in_specs, out_specs=out_specs, out_shape=out_shape,
        compiler_params=pltpu.CompilerParams(dimension_semantics=("parallel",),
                                             vmem_limit_bytes=_vmem_limit(14 * ROWS * d * 4)),
        name="post",
    )(*args)
    return outs


def _conv_body(prev_ref, x_ref, next_ref, w_ref, b_ref, o_ref, xp_ref, *, lay, ktaps, act, rt):
    i = pl.program_id(0)
    pos, ln, _ = lay.seq_pos(i)
    xp_ref[0, 0:HALO, :] = jnp.where(pos != 0, prev_ref[...].astype(F32), 0.0)
    xp_ref[0, HALO:HALO + ROWS, :] = x_ref[...].astype(F32)
    xp_ref[0, HALO + ROWS:HALO + ROWS + HALO, :] = jnp.where(pos != ln - 1, next_ref[...].astype(F32), 0.0)
    span = ROWS + 2 * HALO - SUBLANES
    for m in range(1, SUBLANES):
        xp_ref[m, 0:span, :] = xp_ref[0, m:m + span, :]
    pad = ktaps // 2
    tc = o_ref.shape[1]
    for r in range(ROWS // rt):
        acc = jnp.broadcast_to(b_ref[...], (rt, tc))
        for k in range(ktaps):
            start = HALO - pad + k + r * rt
            m = start % SUBLANES
            acc = acc + w_ref[pl.ds(k, 1), :] * xp_ref[m, pl.ds(start - m, rt), :]
        if act == "silu":
            acc = _silu(acc)
        o_ref[r * rt:(r + 1) * rt, :] = acc.astype(o_ref.dtype)


def _dwconv(lay, x, col0, width, w, b, layer, act, out_dtype, tc=512, rt=32):
    ktaps = w.shape[1]
    assert ktaps // 2 <= HALO and col0 % tc == 0 and width % tc == 0
    cb = col0 // tc
    hb = ROWS // HALO
    last = lay.tokens // HALO - 1
    return pl.pallas_call(
        functools.partial(_conv_body, lay=lay, ktaps=ktaps, act=act, rt=rt),
        grid=(lay.nb, width // tc),
        in_specs=[pl.BlockSpec((HALO, tc), lambda i, c: (jnp.maximum(i * hb - 1, 0), c + cb)),
                  pl.BlockSpec((ROWS, tc), lambda i, c: (i, c + cb)),
                  pl.BlockSpec((HALO, tc), lambda i, c: (jnp.minimum((i + 1) * hb, last), c + cb)),
                  pl.BlockSpec((None, ktaps, tc), lambda i, c: (layer, 0, c)),
                  pl.BlockSpec((None, 1, tc), lambda i, c: (layer, 0, c))],
        out_specs=pl.BlockSpec((ROWS, tc), lambda i, c: (i, c)),
        out_shape=jax.ShapeDtypeStruct((lay.tokens, width), out_dtype),
        scratch_shapes=[pltpu.VMEM((SUBLANES, ROWS + 2 * HALO, tc), F32)],
        compiler_params=pltpu.CompilerParams(dimension_semantics=("parallel", "parallel")),
        name="dwconv%d" % ktaps,
    )(x, x, x, w, b)


def _dt_body(x_ref, w_ref, b_ref, dt_ref, dtt_ref):
    a = jnp.dot(x_ref[...], w_ref[...].astype(BF16), preferred_element_type=F32) + b_ref[...]
    dt = _softplus(a)
    dt_ref[...] = dt
    dtt_ref[...] = dt.T


def _dt_proj(h, w_in, layer, col0, bias, tm=512):
    m, kdim = h.shape
    return pl.pallas_call(
        _dt_body,
        grid=(m // tm,),
        in_specs=[pl.BlockSpec((tm, kdim), lambda i: (i, 0)),
                  pl.BlockSpec((None, kdim, LANES), lambda i: (layer, 0, col0 // LANES)),
                  pl.BlockSpec((1, LANES), lambda i: (0, 0))],
        out_specs=[pl.BlockSpec((tm, LANES), lambda i: (i, 0)),
                   pl.BlockSpec((LANES, tm), lambda i: (0, i))],
        out_shape=[jax.ShapeDtypeStruct((m, LANES), F32), jax.ShapeDtypeStruct((LANES, m), F32)],
        compiler_params=pltpu.CompilerParams(dimension_semantics=("parallel",)),
        name="dt_proj",
    )(h, w_in, bias)


def _split3(x):
    hi = x.astype(BF16)
    r1 = x - hi.astype(F32)
    mid = r1.astype(BF16)
    lo = (r1 - mid.astype(F32)).astype(BF16)
    return hi, mid, lo


_NT = (((1,), (1,)), ((), ()))
_TN = (((0,), (0,)), ((), ()))


def _ssd_body(xs_ref, b_ref, c_ref, dt_ref, dtt_ref, arow_ref, acol_ref, h0_ref, y_ref, st_ref, h_scr,
              *, lay, heads_per_group):
    t = SSD_CHUNK
    p = SSD_HEAD_DIM
    d = pl.program_id(0)
    g = pl.program_id(1)
    s = pl.program_id(2)
    blk = jnp.where(d == 0, s, lay.nb - 1 - s)
    pos, ln, is_ctx = lay.seq_pos(blk)
    first = jnp.where(d == 0, pos == 0, pos == ln - 1)
    last = jnp.where(d == 0, pos == ln - 1, pos == 0)

    @pl.when(jnp.logical_and(first, is_ctx))
    def _():
        h_scr[...] = jnp.zeros_like(h_scr)

    @pl.when(jnp.logical_and(first, jnp.logical_not(is_ctx)))
    def _():
        h_scr[...] = h0_ref[...]

    shift = (LANES - (d * (LANES // 2) + g * heads_per_group)) % LANES
    a_c = pltpu.roll(jnp.broadcast_to(arow_ref[...], (8, LANES)), shift, 1)[0:1, :]
    row = lax.broadcasted_iota(jnp.int32, (t, t), 0)
    col = lax.broadcasted_iota(jnp.int32, (t, t), 1)
    mask = (row - col) * (1 - 2 * d) >= 0
    maskb = mask.astype(BF16)
    lane = lax.broadcasted_iota(jnp.int32, (t, LANES), 1)
    low = lane < p

    for j in range(ROWS // t):
        cj = jnp.where(d == 0, j, ROWS // t - 1 - j)
        r0 = pl.multiple_of(cj * t, t)
        x = xs_ref[pl.ds(r0, t), :]
        bm = b_ref[pl.ds(r0, t), :]
        cm = c_ref[pl.ds(r0, t), :]
        dtc = pltpu.roll(dt_ref[pl.ds(r0, t), :], shift, 1)
        dac = dtc * a_c
        dtr = dtt_ref[:, pl.ds(r0, t)]
        dar = dtr * acol_ref[...]
        cs_col = sum(jnp.dot(maskb, part, preferred_element_type=F32) for part in _split3(dac))
        cs_row = sum(lax.dot_general(part, maskb, _NT, preferred_element_type=F32) for part in _split3(dar))
        tot_l = jnp.where(d == 0, cs_col[t - 1:t, :], cs_col[0:1, :])
        tot_r = jnp.sum(dar, axis=1, keepdims=True)
        cb = lax.dot_general(cm, bm, _NT, preferred_element_type=F32)
        e_in = jnp.exp(cs_col)
        w_st = dtc * jnp.exp(tot_l - cs_col)
        ch = lax.dot_general(cm, h_scr[...].astype(BF16), _NT, preferred_element_type=F32)
        xw_parts = []
        for pr in range(heads_per_group // 2):
            xpair = x[:, pr * LANES:(pr + 1) * LANES]
            ypair = None
            for r in (2 * pr, 2 * pr + 1):
                seg = jnp.exp(jnp.where(mask, cs_col[:, r:r + 1] - cs_row[r:r + 1, :], -jnp.inf))
                mh = (cb * seg * dtr[r:r + 1, :]).astype(BF16)
                xh = jnp.where(low if r % 2 == 0 else jnp.logical_not(low), xpair, jnp.zeros_like(xpair))
                yd = jnp.dot(mh, xh, preferred_element_type=F32)
                ypair = yd if ypair is None else ypair + yd
            esc = jnp.where(low, e_in[:, 2 * pr:2 * pr + 1], e_in[:, 2 * pr + 1:2 * pr + 2])
            wsc = jnp.where(low, w_st[:, 2 * pr:2 * pr + 1], w_st[:, 2 * pr + 1:2 * pr + 2])
            y_ref[pl.ds(r0, t), pr * LANES:(pr + 1) * LANES] = ypair + ch[:, pr * LANES:(pr + 1) * LANES] * esc
            xw_parts.append((xpair.astype(F32) * wsc).astype(BF16))
        xw = jnp.concatenate(xw_parts, axis=1)
        st = lax.dot_general(xw, bm, _TN, preferred_element_type=F32)
        dec = jnp.exp(tot_r)
        for r in range(heads_per_group):
            h_scr[r * p:(r + 1) * p, :] = h_scr[r * p:(r + 1) * p, :] * dec[r:r + 1, :] + st[r * p:(r + 1) * p, :]

    @pl.when(jnp.logical_and(last, is_ctx))
    def _():
        st_ref[...] = h_scr[...]


def _ssd(lay, xa, dt, dtt, a_row, a_col, h0, layer, inner):
    r = inner // SSD_HEAD_DIM // SSD_GROUPS
    assert r == 8 and 2 * SSD_GROUPS * r == LANES
    gw = r * SSD_HEAD_DIM
    n = SSD_STATE
    nb = lay.nb

    def blk(d, s):
        return jnp.where(d == 0, s, nb - 1 - s)

    def seq_s(d, s):
        return jnp.clip((blk(d, s) - lay.ncb) // lay.cps, 0, lay.dec_batch - 1)

    return pl.pallas_call(
        functools.partial(_ssd_body, lay=lay, heads_per_group=r),
        grid=(2, SSD_GROUPS, nb),
        in_specs=[pl.BlockSpec((ROWS, gw), lambda d, g, s: (blk(d, s), g)),
                  pl.BlockSpec((ROWS, n), lambda d, g, s: (blk(d, s), inner // n + g)),
                  pl.BlockSpec((ROWS, n), lambda d, g, s: (blk(d, s), inner // n + SSD_GROUPS + g)),
                  pl.BlockSpec((ROWS, LANES), lambda d, g, s: (blk(d, s), 0)),
                  pl.BlockSpec((r, ROWS), lambda d, g, s: (d * SSD_GROUPS + g, blk(d, s))),
                  pl.BlockSpec((1, LANES), lambda d, g, s: (0, 0)),
                  pl.BlockSpec((r, LANES), lambda d, g, s: (d * SSD_GROUPS + g, 0)),
                  pl.BlockSpec((None, None, None, gw, n), lambda d, g, s: (seq_s(d, s), layer, d, g, 0))],
        out_specs=[pl.BlockSpec((None, ROWS, gw), lambda d, g, s: (d, blk(d, s), g)),
                   pl.BlockSpec((None, None, gw, n),
                                lambda d, g, s: (jnp.minimum(blk(d, s), lay.ncb - 1), d, g, 0))],
        out_shape=[jax.ShapeDtypeStruct((2, lay.tokens, inner), F32),
                   jax.ShapeDtypeStruct((lay.ncb, 2, inner, n), F32)],
        scratch_shapes=[pltpu.VMEM((gw, n), F32)],
        compiler_params=pltpu.CompilerParams(dimension_semantics=("arbitrary", "arbitrary", "arbitrary")),
        name="ssd_scan",
    )(xa, xa, xa, dt, dtt, a_row, a_col, h0)


def _gate_body(yf_ref, yb_ref, xs_ref, z_ref, dsk_ref, g_ref, o_ref):
    y = yf_ref[...] + yb_ref[...] + xs_ref[...].astype(F32) * dsk_ref[...]
    y = y * _silu(z_ref[...].astype(F32))
    o_ref[...] = (_rms(y) * g_ref[...]).astype(o_ref.dtype)


def _ssd_gate(lay, y2, xa, zx, dskip, norm, layer, inner):
    return pl.pallas_call(
        _gate_body,
        grid=(lay.nb,),
        in_specs=[pl.BlockSpec((None, ROWS, inner), lambda i: (0, i, 0)),
                  pl.BlockSpec((None, ROWS, inner), lambda i: (1, i, 0)),
                  _row_spec(inner), _row_spec(inner), _vec_spec(layer, inner), _vec_spec(layer, inner)],
        out_specs=_row_spec(inner),
        out_shape=jax.ShapeDtypeStruct((lay.tokens, inner), BF16),
        compiler_params=pltpu.CompilerParams(dimension_semantics=("parallel",),
                                             vmem_limit_bytes=_vmem_limit(12 * ROWS * inner * 4)),
        name="ssd_gate",
    )(y2, y2, xa, zx, dskip, norm)


def _ln_silu_body(x_ref, g_ref, b_ref, o_ref):
    x = x_ref[...]
    mu = jnp.mean(x, axis=-1, keepdims=True)
    xc = x - mu
    var = jnp.mean(xc * xc, axis=-1, keepdims=True)
    o_ref[...] = _silu(xc * lax.rsqrt(var + EPS) * g_ref[...] + b_ref[...]).astype(o_ref.dtype)


def _ln_silu(lay, x, g, b, layer):
    d = x.shape[1]
    return pl.pallas_call(
        _ln_silu_body,
        grid=(lay.nb,),
        in_specs=[_row_spec(d), _vec_spec(layer, d), _vec_spec(layer, d)],
        out_specs=_row_spec(d),
        out_shape=jax.ShapeDtypeStruct(x.shape, BF16),
        compiler_params=pltpu.CompilerParams(dimension_semantics=("parallel",),
                                             vmem_limit_bytes=_vmem_limit(8 * ROWS * d * 4)),
        name="cfm_ln",
    )(x, g, b)


def _gmlp_body(u_ref, v_ref, g_ref, b_ref, ws_ref, bs_ref, o_ref, vn_ref, *, groups):
    v = v_ref[...].astype(F32)
    mu = jnp.mean(v, axis=-1, keepdims=True)
    vc = v - mu
    var = jnp.mean(vc * vc, axis=-1, keepdims=True)
    vn_ref[...] = (vc * lax.rsqrt(var + EPS) * g_ref[...] + b_ref[...]).astype(BF16)
    t = GM_CHUNK
    gk = v.shape[1] // groups
    for gi in range(groups):
        wg = ws_ref[gi].astype(BF16)
        bias = bs_ref[:, gi:gi + 1]
        for c in range(ROWS // t):
            rs = slice(c * t, (c + 1) * t)
            cs = slice(gi * gk, (gi + 1) * gk)
            sv = jnp.dot(wg, vn_ref[rs, cs], preferred_element_type=F32) + bias
            o_ref[rs, cs] = (u_ref[rs, cs].astype(F32) * sv).astype(o_ref.dtype)


def _gmlp_gate(lay, uv, ln_g, ln_b, w_s, b_st, layer):
    dim = uv.shape[1] // 2
    groups = w_s.shape[1]
    return pl.pallas_call(
        functools.partial(_gmlp_body, groups=groups),
        grid=(lay.nb,),
        in_specs=[pl.BlockSpec((ROWS, dim), lambda i: (i, 0)),
                  pl.BlockSpec((ROWS, dim), lambda i: (i, 1)),
                  _vec_spec(layer, dim), _vec_spec(layer, dim),
                  pl.BlockSpec((None, groups, GM_CHUNK, GM_CHUNK), lambda i: (layer, 0, 0, 0)),
                  pl.BlockSpec((None, GM_CHUNK, groups), lambda i: (layer, 0, 0))],
        out_specs=_row_spec(dim),
        out_shape=jax.ShapeDtypeStruct((lay.tokens, dim), BF16),
        scratch_shapes=[pltpu.VMEM((ROWS, dim), BF16)],
        compiler_params=pltpu.CompilerParams(dimension_semantics=("parallel",),
                                             vmem_limit_bytes=_vmem_limit(12 * ROWS * dim * 4)),
        name="gmlp_gate",
    )(uv, uv, ln_g, ln_b, w_s, b_st)


def _gather_body(idx_ref, src_ref, o_ref, buf, sem, *, n_add, m):
    rb = o_ref.shape[0]
    b = pl.program_id(0)

    def row_copy(a, r):
        row = idx_ref[a * m + b * rb + r]
        return pltpu.make_async_copy(src_ref.at[pl.ds(row, 1), :], buf.at[a, pl.ds(r, 1), :], sem.at[a])

    for a in range(n_add):
        def start(r, carry, a=a):
            row_copy(a, r).start()
            return carry
        lax.fori_loop(0, rb, start, 0)
    for a in range(n_add):
        def wait(r, carry, a=a):
            row_copy(a, r).wait()
            return carry
        lax.fori_loop(0, rb, wait, 0)
    acc = buf[0]
    for a in range(1, n_add):
        acc = acc + buf[a]
    o_ref[...] = acc.astype(o_ref.dtype)


def _row_gather(src, idx, n_add, out_dtype, name):
    m = idx.shape[0] // n_add
    d = src.shape[1]
    rb = MOE_ROWS
    assert m % rb == 0 and src.dtype == F32
    return pl.pallas_call(
        functools.partial(_gather_body, n_add=n_add, m=m),
        grid_spec=pltpu.PrefetchScalarGridSpec(
            num_scalar_prefetch=1,
            grid=(m // rb,),
            in_specs=[pl.BlockSpec(memory_space=pl.ANY)],
            out_specs=pl.BlockSpec((rb, d), lambda b, idx_ref: (b, 0)),
            scratch_shapes=[pltpu.VMEM((n_add, rb, d), F32), pltpu.SemaphoreType.DMA((n_add,))]),
        out_shape=jax.ShapeDtypeStruct((m, d), out_dtype),
        compiler_params=pltpu.CompilerParams(dimension_semantics=("arbitrary",),
                                             vmem_limit_bytes=_vmem_limit((2 * n_add + 4) * rb * d * 4)),
        name=name,
    )(idx, src)


def _moe_mm_body(be_ref, bs_ref, nu_ref, *refs, n_w, has_scale):
    it = iter(refs)
    x_ref = next(it)
    w_refs = [next(it) for _ in range(n_w)]
    scale_ref = next(it) if has_scale else None
    o_ref = next(it)
    wb_refs = [next(it) for _ in range(n_w)]
    del bs_ref
    b = pl.program_id(1)
    valid = b < nu_ref[0]
    changed = jnp.logical_or(b == 0, be_ref[b] != be_ref[jnp.maximum(b - 1, 0)])

    @pl.when(jnp.logical_and(valid, changed))
    def _():
        for w_ref, wb in zip(w_refs, wb_refs):
            wb[...] = w_ref[...].astype(BF16)

    @pl.when(valid)
    def _():
        x = x_ref[...]
        a = jnp.dot(x, wb_refs[0][...], preferred_element_type=F32)
        if n_w == 2:
            a = _silu(a) * jnp.dot(x, wb_refs[1][...], preferred_element_type=F32)
        if has_scale:
            a = a * scale_ref[...]
        o_ref[...] = a.astype(o_ref.dtype)

    @pl.when(jnp.logical_not(valid))
    def _():
        o_ref[...] = jnp.zeros_like(o_ref)


def _moe_mm(x, ws, layer, n_out, tn, blk_expert, blk_src, n_used, out_dtype, scale=None, name="moe_mm"):
    p, kdim = x.shape
    rb = MOE_ROWS
    n_w = len(ws)
    in_specs = [pl.BlockSpec((rb, kdim), lambda j, b, be, bs, nu: (bs[b], 0))]
    in_specs += [pl.BlockSpec((None, None, kdim, tn), lambda j, b, be, bs, nu: (layer, be[b], 0, j)) for _ in ws]
    args = [x] + list(ws)
    if scale is not None:
        in_specs.append(pl.BlockSpec((rb, 1), lambda j, b, be, bs, nu: (bs[b], 0)))
        args.append(scale)
    osize = jnp.dtype(out_dtype).itemsize
    est = 2 * rb * kdim * 2 + n_w * (2 * kdim * tn * 4 + kdim * tn * 2 + rb * tn * 4) + 2 * rb * tn * osize
    return pl.pallas_call(
        functools.partial(_moe_mm_body, n_w=n_w, has_scale=scale is not None),
        grid_spec=pltpu.PrefetchScalarGridSpec(
            num_scalar_prefetch=3,
            grid=(n_out // tn, p // rb),
            in_specs=in_specs,
            out_specs=pl.BlockSpec((rb, tn), lambda j, b, be, bs, nu: (b, j)),
            scratch_shapes=[pltpu.VMEM((kdim, tn), BF16) for _ in ws]),
        out_shape=jax.ShapeDtypeStruct((p, n_out), out_dtype),
        compiler_params=pltpu.CompilerParams(dimension_semantics=("arbitrary", "arbitrary"),
                                             vmem_limit_bytes=_vmem_limit(est)),
        name=name,
    )(blk_expert, blk_src, n_used, *args)


def _moe_routing(route, n_exp):
    t = route.shape[0]
    rb = MOE_ROWS
    e_flat = jnp.concatenate([route[:, 0], route[:, 1]]).astype(jnp.int32)
    g_flat = jnp.concatenate([route[:, 2], route[:, 3]])
    tok = jnp.concatenate([jnp.arange(t, dtype=jnp.int32)] * TOP_K)
    onehot = (e_flat[:, None] == jnp.arange(n_exp, dtype=jnp.int32)[None, :]).astype(jnp.int32)
    csum = jnp.cumsum(onehot, axis=0)
    rank = jnp.sum(csum * onehot, axis=1) - 1
    counts = csum[-1]
    padded = (counts + rb - 1) // rb * rb
    pend = jnp.cumsum(padded)
    pstart = pend - padded
    ppos = pstart[e_flat] + rank
    p = TOP_K * t + n_exp * rb
    src = jnp.zeros((p,), jnp.int32).at[ppos].set(tok)
    gate = jnp.zeros((p, 1), F32).at[ppos, 0].set(g_flat)
    n_used = (pend[-1] // rb).astype(jnp.int32)
    blk_src = jnp.minimum(jnp.arange(p // rb, dtype=jnp.int32), n_used - 1)
    blk_expert = jnp.minimum(jnp.sum(pend[None, :] <= (blk_src * rb)[:, None], axis=1), n_exp - 1).astype(jnp.int32)
    return src, gate, ppos.astype(jnp.int32), blk_expert, blk_src, n_used.reshape(1)


def _grid_pos_embed(n_tokens, dim):
    rows = n_tokens // GRID_W
    quarter = dim // 4
    omega = 1.0 / (10000.0 ** (jnp.arange(quarter, dtype=F32) / quarter))

    def axis_embed(n):
        ang = jnp.arange(n, dtype=F32)[:, None] * omega[None, :]
        return jnp.concatenate([jnp.sin(ang), jnp.cos(ang)], axis=-1)

    er = jnp.broadcast_to(axis_embed(rows)[:, None, :], (rows, GRID_W, dim // 2))
    ec = jnp.broadcast_to(axis_embed(GRID_W)[None, :, :], (rows, GRID_W, dim // 2))
    return jnp.concatenate([er, ec], axis=-1).reshape(n_tokens, dim)


def _v3(a):
    return a.reshape(a.shape[0], 1, a.shape[1])


def kernel(x_prompt, x_sample, state_ssd, c, c_ctx, w_mod, b_mod, norm_mix_pre, norm_mix_post, norm_ffn_pre, norm_ffn_post, w_in_even, ssd_conv_w, ssd_conv_b, ssd_dt_bias, ssd_a_log, ssd_d, ssd_norm, cfm_conv_w, cfm_conv_b, cfm_ln_g, cfm_ln_b, w_out_even, w_in_odd, gm_ln_g, gm_ln_b, gm_w_s, gm_b_s, w_out_odd, ffn_w1, ffn_w3, ffn_w2, router_w, moe_w1, moe_w3, moe_w2):
    batch, seq, d = x_prompt.shape
    dec_batch, dec_seq, _ = x_sample.shape
    depth = w_mod.shape[0]
    lay = _Layout(batch, seq, dec_batch, dec_seq)
    tokens = lay.tokens
    inner = d
    heads = inner // SSD_HEAD_DIM
    bc_dim = SSD_GROUPS * SSD_STATE
    conv_dim = inner + 2 * bc_dim
    cfm_dim = cfm_conv_w.shape[2]
    dt_col = inner + conv_dim
    glu_col = dt_col + 2 * heads
    n_exp = router_w.shape[2]
    d_ff = ffn_w1.shape[2]
    d_ffe = moe_w1.shape[3]
    tm = 1024 if tokens % 1024 == 0 else 512
    tm_long = 768 if tokens % 768 == 0 else 512

    n_cond = 1 + dec_batch
    cond = jnp.zeros((8, d), F32).at[0].set(c_ctx).at[1:n_cond].set(c)
    mod = jnp.stack([
        _mm([cond], [(w_mod, (l,), 0)], N_MOD * d, tm=8, tn=512, out_dtype=F32,
            bias=b_mod[l][None, :], x_pro="silu", name="adaln_mod")
        for l in range(depth)])
    mod = mod.reshape(depth, 8, 1, N_MOD * d)

    x = jnp.concatenate([x_prompt.reshape(batch * seq, d), x_sample.reshape(dec_batch * dec_seq, d)], axis=0)
    pos = _grid_pos_embed(dec_seq, d)
    g_mix_pre, g_mix_post = _v3(norm_mix_pre), _v3(norm_mix_post)
    g_ffn_pre, g_ffn_post = _v3(norm_ffn_pre), _v3(norm_ffn_post)
    x, h = _entry(lay, x, pos, g_mix_pre, mod)

    a_neg = -jnp.exp(ssd_a_log.astype(F32))
    h0_all = state_ssd.reshape(dec_batch, state_ssd.shape[1], 2, inner, SSD_STATE)
    dskip = _v3(jnp.repeat(ssd_d, SSD_HEAD_DIM, axis=1))
    glu_w = w_in_even[:, :, glu_col:]
    states = []
    for l in range(depth):
        i = l // 2
        if l % 2 == 0:
            zx = _mm([h], [(w_in_even, (i,), 0)], dt_col, tm=tm, tn=512, name="even_in_zx")
            dt, dtt = _dt_proj(h, w_in_even, i, dt_col, ssd_dt_bias[i].reshape(1, 2 * heads))
            u0 = _mm([h], [(glu_w, (i,), 0), (glu_w, (i,), cfm_dim)], cfm_dim, tm=tm, tn=256,
                     act="glu", name="even_in_glu")
            xa = _dwconv(lay, zx, inner, conv_dim, ssd_conv_w, _v3(ssd_conv_b), i, "silu", BF16)
            a_row = a_neg[i].reshape(1, 2 * heads)
            a_col = jnp.broadcast_to(a_neg[i].reshape(2 * heads, 1), (2 * heads, LANES))
            y2, st = _ssd(lay, xa, dt, dtt, a_row, a_col, h0_all, i, inner)
            states.append(st)
            yg = _ssd_gate(lay, y2, xa, zx, dskip, _v3(ssd_norm), i, inner)
            uc = _dwconv(lay, u0, 0, cfm_dim, cfm_conv_w, _v3(cfm_conv_b), i, "none", F32)
            ug = _ln_silu(lay, uc, _v3(cfm_ln_g), _v3(cfm_ln_b), i)
            y = _mm([yg, ug], [(w_out_even, (i,), 0)], d, tm=tm, tn=256, out_dtype=F32, x_buffers=1,
                    name="even_out")
        else:
            uv = _mm([h], [(w_in_odd, (i,), 0)], w_in_odd.shape[2], tm=tm, tn=512, act="gelu",
                     name="odd_in")
            gated = _gmlp_gate(lay, uv, _v3(gm_ln_g), _v3(gm_ln_b), gm_w_s,
                               jnp.swapaxes(gm_b_s, 1, 2), i)
            y = _mm([gated], [(w_out_odd, (i,), 0)], d, tm=tm, tn=256, out_dtype=F32, x_buffers=1,
                    name="odd_out")

        if l % 2 == 0:
            x, h = _post(lay, x, y, g_mix_post, mod, l, 2, pre=(g_ffn_pre, l, 4, 3))
            act = _mm([h], [(ffn_w1, (i,), 0), (ffn_w3, (i,), 0)], d_ff, tm=tm, tn=256, act="swiglu",
                      name="ffn_up")
            y = _mm([act], [(ffn_w2, (i,), 0)], d, tm=tm_long, tn=256, out_dtype=F32, x_buffers=1,
                    name="ffn_down")
        else:
            rw = jnp.zeros((d, LANES), F32).at[:, :n_exp].set(router_w[i])
            x, h, h32, route = _post(lay, x, y, g_mix_post, mod, l, 2, pre=(g_ffn_pre, l, 4, 3), router=rw,
                                     n_exp=n_exp)
            del h
            src, gate, ppos, blk_expert, blk_src, n_used = _moe_routing(route, n_exp)
            xs = _row_gather(h32, src, 1, BF16, "moe_gather")
            tn_e = 512 if d_ffe % 512 == 0 else 256
            act = _moe_mm(xs, [moe_w1, moe_w3], i, d_ffe, tn_e, blk_expert, blk_src, n_used, BF16,
                          name="moe_up")
            ys = _moe_mm(act, [moe_w2], i, d, 512, blk_expert, blk_src, n_used, F32, scale=gate,
                         name="moe_down")
            y = _row_gather(ys, ppos, TOP_K, F32, "moe_combine")

        if l + 1 < depth:
            x, h = _post(lay, x, y, g_ffn_post, mod, l, 5, pre=(g_mix_pre, l + 1, 1, 0))
        else:
            (x,) = _post(lay, x, y, g_ffn_post, mod, l, 5)

    n_ctx = batch * seq
    y_prompt = x[:n_ctx].reshape(batch, seq, d)
    y_sample = x[n_ctx:].reshape(dec_batch, dec_seq, d)
    state_new = jnp.stack(states, axis=1).reshape(batch, len(states), 2, heads, SSD_HEAD_DIM, SSD_STATE)
    return (y_prompt, y_sample, state_new.astype(state_ssd.dtype))
```

```python
import functools
import math

import jax
import jax.numpy as jnp
from jax import lax
from jax.experimental import pallas as pl
from jax.experimental.pallas import tpu as pltpu

F32 = jnp.float32
BF16 = jnp.bfloat16
EPS = 1e-6
ROWS = 256
HALO = 16
LANES = 128
SUBLANES = 8
MOE_ROWS = 512
MOE_DOWN_ROWS = 256
GATHER_ROWS = 256
GATHER_UNROLL = 8
V7X_VMEM_BYTES = 64 * 1024 * 1024
VMEM_CAP = V7X_VMEM_BYTES - 6 * 1024 * 1024
N_MOD = 6
GRID_W = 64
SSD_CHUNK = 128
SSD_STATE = 128
SSD_HEAD_DIM = 64
SSD_GROUPS = 8
GM_CHUNK = 128
TOP_K = 2


def _vmem_limit(nbytes):
    return int(min(VMEM_CAP, max(32 * 1024 * 1024, nbytes * 5 // 4 + (4 << 20))))


def _sigmoid(x):
    return 1.0 / (1.0 + jnp.exp(-x))


def _silu(x):
    return x * _sigmoid(x)


def _gelu_tanh(x):
    return 0.5 * x * (1.0 + jnp.tanh(math.sqrt(2.0 / math.pi) * (x + 0.044715 * (x * x * x))))


def _softplus(x):
    return jnp.maximum(x, 0.0) + jnp.log1p(jnp.exp(-jnp.abs(x)))


class _Layout:
    def __init__(self, batch, seq, dec_batch, dec_seq):
        assert seq == ROWS and dec_seq % ROWS == 0
        self.ncb = batch
        self.cps = dec_seq // ROWS
        self.nb = batch + dec_batch * self.cps
        self.dec_batch = dec_batch
        self.tokens = self.nb * ROWS

    def seq_pos(self, blk):
        is_ctx = blk < self.ncb
        pos = jnp.where(is_ctx, 0, (blk - self.ncb) % self.cps)
        ln = jnp.where(is_ctx, 1, self.cps)
        return pos, ln, is_ctx

    def cond_row(self, blk):
        return jnp.where(blk < self.ncb, 0, 1 + (blk - self.ncb) // self.cps)


def _mm_body(*refs, n_x, n_w, act, x_pro, has_bias):
    it = iter(refs)
    x_refs = [next(it) for _ in range(n_x)]
    w_refs = [next(it) for _ in range(n_w)]
    bias_ref = next(it) if has_bias else None
    o_ref = next(it)
    accs = []
    for w_ref in w_refs:
        acc = None
        k0 = 0
        for x_ref in x_refs:
            xv = x_ref[...]
            if x_pro == "silu":
                xv = _silu(xv.astype(F32)).astype(BF16)
            kp = xv.shape[1]
            part = jnp.dot(xv, w_ref[k0:k0 + kp, :].astype(BF16), preferred_element_type=F32)
            acc = part if acc is None else acc + part
            k0 += kp
        accs.append(acc)
    a = accs[0]
    if has_bias:
        a = a + bias_ref[...]
    if act == "gelu":
        a = _gelu_tanh(a)
    elif act == "swiglu":
        a = _silu(a) * accs[1]
    elif act == "glu":
        a = a * _sigmoid(accs[1])
    o_ref[...] = a.astype(o_ref.dtype)


def _mm(xs, ws, n_out, *, tm, tn, act="none", out_dtype=BF16, bias=None, x_pro=None, x_buffers=2,
        name="mm"):
    m = xs[0].shape[0]
    kdim = sum(x.shape[1] for x in xs)
    n_w = len(ws)
    assert m % tm == 0 and n_out % tn == 0
    in_specs = []
    for x in xs:
        if x_buffers == 1:
            in_specs.append(pl.BlockSpec((tm, x.shape[1]), lambda i, j: (i, 0), pipeline_mode=pl.Buffered(1)))
        else:
            in_specs.append(pl.BlockSpec((tm, x.shape[1]), lambda i, j: (i, 0)))
    args = list(xs)
    for w, lead, col0 in ws:
        assert col0 % tn == 0 and w.shape[-2] == kdim
        in_specs.append(pl.BlockSpec((None,) * len(lead) + (kdim, tn),
                                     functools.partial(lambda i, j, lead, cb: lead + (0, j + cb),
                                                       lead=tuple(lead), cb=col0 // tn)))
        args.append(w)
    if bias is not None:
        in_specs.append(pl.BlockSpec((1, tn), lambda i, j: (0, j)))
        args.append(bias)
    osize = jnp.dtype(out_dtype).itemsize
    est = (x_buffers * tm * kdim * xs[0].dtype.itemsize + n_w * (2 * kdim * tn * 4 + kdim * tn * 2 + 2 * tm * tn * 4)
           + 2 * tm * tn * osize)
    body = functools.partial(_mm_body, n_x=len(xs), n_w=n_w, act=act, x_pro=x_pro, has_bias=bias is not None)
    return pl.pallas_call(
        body,
        grid=(m // tm, n_out // tn),
        in_specs=in_specs,
        out_specs=pl.BlockSpec((tm, tn), lambda i, j: (i, j)),
        out_shape=jax.ShapeDtypeStruct((m, n_out), out_dtype),
        compiler_params=pltpu.CompilerParams(
            dimension_semantics=("parallel", "parallel"),
            vmem_limit_bytes=_vmem_limit(est)),
        name=name,
    )(*args)


def _rms(x):
    return x * lax.rsqrt(jnp.mean(x * x, axis=-1, keepdims=True) + EPS)


def _mod_spec(lay, layer, which, d):
    return pl.BlockSpec((None, None, 1, d), lambda i: (layer, lay.cond_row(i), 0, which))


def _vec_spec(layer, d):
    return pl.BlockSpec((None, 1, d), lambda i: (layer, 0, 0))


def _row_spec(d):
    return pl.BlockSpec((ROWS, d), lambda i: (i, 0))


def _entry_body(x_ref, pos_ref, g_ref, sc_ref, sh_ref, xo_ref, h_ref, *, lay):
    i = pl.program_id(0)
    x = x_ref[...] + jnp.where(i >= lay.ncb, pos_ref[...], 0.0)
    xo_ref[...] = x
    h_ref[...] = (_rms(x) * g_ref[...] * (1.0 + sc_ref[...]) + sh_ref[...]).astype(BF16)


def _entry(lay, x, pos, g_pre, mod):
    d = x.shape[1]
    cps = lay.cps
    return pl.pallas_call(
        functools.partial(_entry_body, lay=lay),
        grid=(lay.nb,),
        in_specs=[_row_spec(d),
                  pl.BlockSpec((ROWS, d), lambda i: (jnp.maximum(i - lay.ncb, 0) % cps, 0)),
                  _vec_spec(0, d), _mod_spec(lay, 0, 1, d), _mod_spec(lay, 0, 0, d)],
        out_specs=[_row_spec(d), _row_spec(d)],
        out_shape=[jax.ShapeDtypeStruct(x.shape, F32), jax.ShapeDtypeStruct(x.shape, BF16)],
        compiler_params=pltpu.CompilerParams(dimension_semantics=("parallel",),
                                             vmem_limit_bytes=_vmem_limit(10 * ROWS * d * 4)),
        name="entry",
    )(x, pos, g_pre, mod, mod)


def _post_body(*refs, pre, route, n_exp):
    it = iter(refs)
    x_ref, y_ref, gpost_ref, gate_ref = next(it), next(it), next(it), next(it)
    if pre:
        gpre_ref, sc_ref, sh_ref = next(it), next(it), next(it)
    if route:
        rw_ref = next(it)
    xo_ref = next(it)
    x = x_ref[...] + gate_ref[...] * (_rms(y_ref[...]) * gpost_ref[...])
    xo_ref[...] = x
    if not pre:
        return
    h_ref = next(it)
    h = _rms(x) * gpre_ref[...] * (1.0 + sc_ref[...]) + sh_ref[...]
    h_ref[...] = h.astype(BF16)
    if route:
        hf_ref = next(it)
        route_ref = next(it)
        hf_ref[...] = h
        logits = jnp.dot(h, rw_ref[...], preferred_element_type=F32, precision=lax.Precision.HIGHEST)
        lane = lax.broadcasted_iota(jnp.int32, logits.shape, 1)
        neg = jnp.float32(-jnp.inf)
        l1 = jnp.where(lane < n_exp, logits, neg)
        m1 = jnp.max(l1, axis=-1, keepdims=True)
        i1 = jnp.min(jnp.where(l1 == m1, lane, LANES), axis=-1, keepdims=True)
        l2 = jnp.where(lane == i1, neg, l1)
        m2 = jnp.max(l2, axis=-1, keepdims=True)
        i2 = jnp.min(jnp.where(l2 == m2, lane, LANES), axis=-1, keepdims=True)
        e2 = jnp.exp(m2 - m1)
        g1 = 1.0 / (1.0 + e2)
        g2 = e2 / (1.0 + e2)
        route_ref[...] = jnp.where(lane == 0, i1.astype(F32),
                                   jnp.where(lane == 1, i2.astype(F32),
                                             jnp.where(lane == 2, g1, jnp.where(lane == 3, g2, 0.0))))


def _post(lay, x, y, g_post, mod, layer, gate_idx, pre=None, router=None, n_exp=0):
    d = x.shape[1]
    in_specs = [_row_spec(d), _row_spec(d), _vec_spec(layer, d), _mod_spec(lay, layer, gate_idx, d)]
    args = [x, y, g_post, mod]
    out_specs = [_row_spec(d)]
    out_shape = [jax.ShapeDtypeStruct(x.shape, F32)]
    if pre is not None:
        g_pre, l2, sc_i, sh_i = pre
        in_specs += [_vec_spec(l2, d), _mod_spec(lay, l2, sc_i, d), _mod_spec(lay, l2, sh_i, d)]
        args += [g_pre, mod, mod]
        out_specs.append(_row_spec(d))
        out_shape.append(jax.ShapeDtypeStruct(x.shape, BF16))
    if router is not None:
        in_specs.append(pl.BlockSpec((d, LANES), lambda i: (0, 0)))
        args.append(router)
        out_specs += [_row_spec(d), _row_spec(LANES)]
        out_shape += [jax.ShapeDtypeStruct(x.shape, F32), jax.ShapeDtypeStruct((x.shape[0], LANES), F32)]
    outs = pl.pallas_call(
        functools.partial(_post_body, pre=pre is not None, route=router is not None, n_exp=n_exp),
        grid=(lay.nb,),
        in_specs=in_specs, out_specs=out_specs, out_shape=out_shape,
        compiler_params=pltpu.CompilerParams(dimension_semantics=("parallel",),
                                             vmem_limit_bytes=_vmem_limit(14 * ROWS * d * 4)),
        name="post",
    )(*args)
    return outs


def _conv_body(prev_ref, x_ref, next_ref, w_ref, b_ref, o_ref, xp_ref, *, lay, ktaps, act, rt):
    i = pl.program_id(0)
    pos, ln, _ = lay.seq_pos(i)
    xp_ref[0, 0:HALO, :] = jnp.where(pos != 0, prev_ref[...].astype(F32), 0.0)
    xp_ref[0, HALO:HALO + ROWS, :] = x_ref[...].astype(F32)
    xp_ref[0, HALO + ROWS:HALO + ROWS + HALO, :] = jnp.where(pos != ln - 1, next_ref[...].astype(F32), 0.0)
    span = ROWS + 2 * HALO - SUBLANES
    for m in range(1, SUBLANES):
        xp_ref[m, 0:span, :] = xp_ref[0, m:m + span, :]
    pad = ktaps // 2
    tc = o_ref.shape[1]
    for r in range(ROWS // rt):
        acc = jnp.broadcast_to(b_ref[...], (rt, tc))
        for k in range(ktaps):
            start = HALO - pad + k + r * rt
            m = start % SUBLANES
            acc = acc + w_ref[pl.ds(k, 1), :] * xp_ref[m, pl.ds(start - m, rt), :]
        if act == "silu":
            acc = _silu(acc)
        o_ref[r * rt:(r + 1) * rt, :] = acc.astype(o_ref.dtype)


def _dwconv(lay, x, col0, width, w, b, layer, act, out_dtype, tc=512, rt=32):
    ktaps = w.shape[1]
    assert ktaps // 2 <= HALO and col0 % tc == 0 and width % tc == 0
    cb = col0 // tc
    hb = ROWS // HALO
    last = lay.tokens // HALO - 1
    return pl.pallas_call(
        functools.partial(_conv_body, lay=lay, ktaps=ktaps, act=act, rt=rt),
        grid=(lay.nb, width // tc),
        in_specs=[pl.BlockSpec((HALO, tc), lambda i, c: (jnp.maximum(i * hb - 1, 0), c + cb)),
                  pl.BlockSpec((ROWS, tc), lambda i, c: (i, c + cb)),
                  pl.BlockSpec((HALO, tc), lambda i, c: (jnp.minimum((i + 1) * hb, last), c + cb)),
                  pl.BlockSpec((None, ktaps, tc), lambda i, c: (layer, 0, c)),
                  pl.BlockSpec((None, 1, tc), lambda i, c: (layer, 0, c))],
        out_specs=pl.BlockSpec((ROWS, tc), lambda i, c: (i, c)),
        out_shape=jax.ShapeDtypeStruct((lay.tokens, width), out_dtype),
        scratch_shapes=[pltpu.VMEM((SUBLANES, ROWS + 2 * HALO, tc), F32)],
        compiler_params=pltpu.CompilerParams(dimension_semantics=("parallel", "parallel")),
        name="dwconv%d" % ktaps,
    )(x, x, x, w, b)


def _dt_body(x_ref, w_ref, b_ref, dt_ref, dtt_ref):
    a = jnp.dot(x_ref[...], w_ref[...].astype(BF16), preferred_element_type=F32) + b_ref[...]
    dt = _softplus(a)
    dt_ref[...] = dt
    dtt_ref[...] = dt.T


def _dt_proj(h, w_in, layer, col0, bias, tm=512):
    m, kdim = h.shape
    return pl.pallas_call(
        _dt_body,
        grid=(m // tm,),
        in_specs=[pl.BlockSpec((tm, kdim), lambda i: (i, 0)),
                  pl.BlockSpec((None, kdim, LANES), lambda i: (layer, 0, col0 // LANES)),
                  pl.BlockSpec((1, LANES), lambda i: (0, 0))],
        out_specs=[pl.BlockSpec((tm, LANES), lambda i: (i, 0)),
                   pl.BlockSpec((LANES, tm), lambda i: (0, i))],
        out_shape=[jax.ShapeDtypeStruct((m, LANES), F32), jax.ShapeDtypeStruct((LANES, m), F32)],
        compiler_params=pltpu.CompilerParams(dimension_semantics=("parallel",)),
        name="dt_proj",
    )(h, w_in, bias)


def _split3(x):
    hi = x.astype(BF16)
    r1 = x - hi.astype(F32)
    mid = r1.astype(BF16)
    lo = (r1 - mid.astype(F32)).astype(BF16)
    return hi, mid, lo


_NT = (((1,), (1,)), ((), ()))
_TN = (((0,), (0,)), ((), ()))


def _ssd_body(xs_ref, b_ref, c_ref, dt_ref, dtt_ref, arow_ref, acol_ref, h0_ref, y_ref, st_ref, h_scr,
              *, lay, heads_per_group):
    t = SSD_CHUNK
    p = SSD_HEAD_DIM
    d = pl.program_id(0)
    g = pl.program_id(1)
    s = pl.program_id(2)
    blk = jnp.where(d == 0, s, lay.nb - 1 - s)
    pos, ln, is_ctx = lay.seq_pos(blk)
    first = jnp.where(d == 0, pos == 0, pos == ln - 1)
    last = jnp.where(d == 0, pos == ln - 1, pos == 0)

    @pl.when(jnp.logical_and(first, is_ctx))
    def _():
        h_scr[...] = jnp.zeros_like(h_scr)

    @pl.when(jnp.logical_and(first, jnp.logical_not(is_ctx)))
    def _():
        h_scr[...] = h0_ref[...]

    shift = (LANES - (d * (LANES // 2) + g * heads_per_group)) % LANES
    a_c = pltpu.roll(jnp.broadcast_to(arow_ref[...], (8, LANES)), shift, 1)[0:1, :]
    row = lax.broadcasted_iota(jnp.int32, (t, t), 0)
    col = lax.broadcasted_iota(jnp.int32, (t, t), 1)
    mask = (row - col) * (1 - 2 * d) >= 0
    maskb = mask.astype(BF16)
    lane = lax.broadcasted_iota(jnp.int32, (t, LANES), 1)
    low = lane < p

    for j in range(ROWS // t):
        cj = jnp.where(d == 0, j, ROWS // t - 1 - j)
        r0 = pl.multiple_of(cj * t, t)
        x = xs_ref[pl.ds(r0, t), :]
        bm = b_ref[pl.ds(r0, t), :]
        cm = c_ref[pl.ds(r0, t), :]
        dtc = pltpu.roll(dt_ref[pl.ds(r0, t), :], shift, 1)
        dac = dtc * a_c
        dtr = dtt_ref[:, pl.ds(r0, t)]
        dar = dtr * acol_ref[...]
        cs_col = sum(jnp.dot(maskb, part, preferred_element_type=F32) for part in _split3(dac))
        cs_row = sum(lax.dot_general(part, maskb, _NT, preferred_element_type=F32) for part in _split3(dar))
        tot_l = jnp.where(d == 0, cs_col[t - 1:t, :], cs_col[0:1, :])
        tot_r = jnp.sum(dar, axis=1, keepdims=True)
        cb = lax.dot_general(cm, bm, _NT, preferred_element_type=F32)
        e_in = jnp.exp(cs_col)
        w_st = dtc * jnp.exp(tot_l - cs_col)
        ch = lax.dot_general(cm, h_scr[...].astype(BF16), _NT, preferred_element_type=F32)
        xw_parts = []
        for pr in range(heads_per_group // 2):
            xpair = x[:, pr * LANES:(pr + 1) * LANES]
            ypair = None
            for r in (2 * pr, 2 * pr + 1):
                seg = jnp.exp(jnp.where(mask, cs_col[:, r:r + 1] - cs_row[r:r + 1, :], -jnp.inf))
                mh = (cb * seg * dtr[r:r + 1, :]).astype(BF16)
                xh = jnp.where(low if r % 2 == 0 else jnp.logical_not(low), xpair, jnp.zeros_like(xpair))
                yd = jnp.dot(mh, xh, preferred_element_type=F32)
                ypair = yd if ypair is None else ypair + yd
            esc = jnp.where(low, e_in[:, 2 * pr:2 * pr + 1], e_in[:, 2 * pr + 1:2 * pr + 2])
            wsc = jnp.where(low, w_st[:, 2 * pr:2 * pr + 1], w_st[:, 2 * pr + 1:2 * pr + 2])
            y_ref[pl.ds(r0, t), pr * LANES:(pr + 1) * LANES] = ypair + ch[:, pr * LANES:(pr + 1) * LANES] * esc
            xw_parts.append((xpair.astype(F32) * wsc).astype(BF16))
        xw = jnp.concatenate(xw_parts, axis=1)
        st = lax.dot_general(xw, bm, _TN, preferred_element_type=F32)
        dec = jnp.exp(tot_r)
        for r in range(heads_per_group):
            h_scr[r * p:(r + 1) * p, :] = h_scr[r * p:(r + 1) * p, :] * dec[r:r + 1, :] + st[r * p:(r + 1) * p, :]

    @pl.when(jnp.logical_and(last, is_ctx))
    def _():
        st_ref[...] = h_scr[...]


def _ssd(lay, xa, dt, dtt, a_row, a_col, h0, layer, inner):
    r = inner // SSD_HEAD_DIM // SSD_GROUPS
    assert r == 8 and 2 * SSD_GROUPS * r == LANES
    gw = r * SSD_HEAD_DIM
    n = SSD_STATE
    nb = lay.nb

    def blk(d, s):
        return jnp.where(d == 0, s, nb - 1 - s)

    def seq_s(d, s):
        return jnp.clip((blk(d, s) - lay.ncb) // lay.cps, 0, lay.dec_batch - 1)

    return pl.pallas_call(
        functools.partial(_ssd_body, lay=lay, heads_per_group=r),
        grid=(2, SSD_GROUPS, nb),
        in_specs=[pl.BlockSpec((ROWS, gw), lambda d, g, s: (blk(d, s), g)),
                  pl.BlockSpec((ROWS, n), lambda d, g, s: (blk(d, s), inner // n + g)),
                  pl.BlockSpec((ROWS, n), lambda d, g, s: (blk(d, s), inner // n + SSD_GROUPS + g)),
                  pl.BlockSpec((ROWS, LANES), lambda d, g, s: (blk(d, s), 0)),
                  pl.BlockSpec((r, ROWS), lambda d, g, s: (d * SSD_GROUPS + g, blk(d, s))),
                  pl.BlockSpec((1, LANES), lambda d, g, s: (0, 0)),
                  pl.BlockSpec((r, LANES), lambda d, g, s: (d * SSD_GROUPS + g, 0)),
                  pl.BlockSpec((None, None, None, gw, n), lambda d, g, s: (seq_s(d, s), layer, d, g, 0))],
        out_specs=[pl.BlockSpec((None, ROWS, gw), lambda d, g, s: (d, blk(d, s), g)),
                   pl.BlockSpec((None, None, gw, n),
                                lambda d, g, s: (jnp.minimum(blk(d, s), lay.ncb - 1), d, g, 0))],
        out_shape=[jax.ShapeDtypeStruct((2, lay.tokens, inner), F32),
                   jax.ShapeDtypeStruct((lay.ncb, 2, inner, n), F32)],
        scratch_shapes=[pltpu.VMEM((gw, n), F32)],
        compiler_params=pltpu.CompilerParams(dimension_semantics=("arbitrary", "arbitrary", "arbitrary")),
        name="ssd_scan",
    )(xa, xa, xa, dt, dtt, a_row, a_col, h0)


def _gate_body(yf_ref, yb_ref, xs_ref, z_ref, dsk_ref, g_ref, o_ref):
    y = yf_ref[...] + yb_ref[...] + xs_ref[...].astype(F32) * dsk_ref[...]
    y = y * _silu(z_ref[...].astype(F32))
    o_ref[...] = (_rms(y) * g_ref[...]).astype(o_ref.dtype)


def _ssd_gate(lay, y2, xa, zx, dskip, norm, layer, inner):
    return pl.pallas_call(
        _gate_body,
        grid=(lay.nb,),
        in_specs=[pl.BlockSpec((None, ROWS, inner), lambda i: (0, i, 0)),
                  pl.BlockSpec((None, ROWS, inner), lambda i: (1, i, 0)),
                  _row_spec(inner), _row_spec(inner), _vec_spec(layer, inner), _vec_spec(layer, inner)],
        out_specs=_row_spec(inner),
        out_shape=jax.ShapeDtypeStruct((lay.tokens, inner), BF16),
        compiler_params=pltpu.CompilerParams(dimension_semantics=("parallel",),
                                             vmem_limit_bytes=_vmem_limit(12 * ROWS * inner * 4)),
        name="ssd_gate",
    )(y2, y2, xa, zx, dskip, norm)


def _ln_silu_body(x_ref, g_ref, b_ref, o_ref):
    x = x_ref[...]
    mu = jnp.mean(x, axis=-1, keepdims=True)
    xc = x - mu
    var = jnp.mean(xc * xc, axis=-1, keepdims=True)
    o_ref[...] = _silu(xc * lax.rsqrt(var + EPS) * g_ref[...] + b_ref[...]).astype(o_ref.dtype)


def _ln_silu(lay, x, g, b, layer):
    d = x.shape[1]
    return pl.pallas_call(
        _ln_silu_body,
        grid=(lay.nb,),
        in_specs=[_row_spec(d), _vec_spec(layer, d), _vec_spec(layer, d)],
        out_specs=_row_spec(d),
        out_shape=jax.ShapeDtypeStruct(x.shape, BF16),
        compiler_params=pltpu.CompilerParams(dimension_semantics=("parallel",),
                                             vmem_limit_bytes=_vmem_limit(8 * ROWS * d * 4)),
        name="cfm_ln",
    )(x, g, b)


def _gmlp_body(u_ref, v_ref, g_ref, b_ref, ws_ref, bs_ref, o_ref, vn_ref, *, groups):
    v = v_ref[...].astype(F32)
    mu = jnp.mean(v, axis=-1, keepdims=True)
    vc = v - mu
    var = jnp.mean(vc * vc, axis=-1, keepdims=True)
    vn_ref[...] = (vc * lax.rsqrt(var + EPS) * g_ref[...] + b_ref[...]).astype(BF16)
    t = GM_CHUNK
    gk = v.shape[1] // groups
    for gi in range(groups):
        wg = ws_ref[gi].astype(BF16)
        bias = bs_ref[:, gi:gi + 1]
        for c in range(ROWS // t):
            rs = slice(c * t, (c + 1) * t)
            cs = slice(gi * gk, (gi + 1) * gk)
            sv = jnp.dot(wg, vn_ref[rs, cs], preferred_element_type=F32) + bias
            o_ref[rs, cs] = (u_ref[rs, cs].astype(F32) * sv).astype(o_ref.dtype)


def _gmlp_gate(lay, uv, ln_g, ln_b, w_s, b_st, layer):
    dim = uv.shape[1] // 2
    groups = w_s.shape[1]
    return pl.pallas_call(
        functools.partial(_gmlp_body, groups=groups),
        grid=(lay.nb,),
        in_specs=[pl.BlockSpec((ROWS, dim), lambda i: (i, 0)),
                  pl.BlockSpec((ROWS, dim), lambda i: (i, 1)),
                  _vec_spec(layer, dim), _vec_spec(layer, dim),
                  pl.BlockSpec((None, groups, GM_CHUNK, GM_CHUNK), lambda i: (layer, 0, 0, 0)),
                  pl.BlockSpec((None, GM_CHUNK, groups), lambda i: (layer, 0, 0))],
        out_specs=_row_spec(dim),
        out_shape=jax.ShapeDtypeStruct((lay.tokens, dim), BF16),
        scratch_shapes=[pltpu.VMEM((ROWS, dim), BF16)],
        compiler_params=pltpu.CompilerParams(dimension_semantics=("parallel",),
                                             vmem_limit_bytes=_vmem_limit(12 * ROWS * dim * 4)),
        name="gmlp_gate",
    )(uv, uv, ln_g, ln_b, w_s, b_st)


def _gather_body(idx_ref, src_ref, o_ref, buf, sem, *, n_add, m):
    rb = o_ref.shape[0]
    b = pl.program_id(0)

    def row_copy(a, r):
        row = idx_ref[a * m + b * rb + r]
        return pltpu.make_async_copy(src_ref.at[pl.ds(row, 1), :], buf.at[a, pl.ds(r, 1), :], sem.at[a])

    for a in range(n_add):
        def start(r, carry, a=a):
            row_copy(a, r).start()
            return carry
        lax.fori_loop(0, rb, start, 0, unroll=GATHER_UNROLL)
    for a in range(n_add):
        def wait(r, carry, a=a):
            row_copy(a, r).wait()
            return carry
        lax.fori_loop(0, rb, wait, 0, unroll=GATHER_UNROLL)
    acc = buf[0]
    for a in range(1, n_add):
        acc = acc + buf[a]
    o_ref[...] = acc.astype(o_ref.dtype)


def _row_gather(src, idx, n_add, out_dtype, name):
    m = idx.shape[0] // n_add
    d = src.shape[1]
    rb = GATHER_ROWS
    assert m % rb == 0 and src.dtype == F32
    return pl.pallas_call(
        functools.partial(_gather_body, n_add=n_add, m=m),
        grid_spec=pltpu.PrefetchScalarGridSpec(
            num_scalar_prefetch=1,
            grid=(m // rb,),
            in_specs=[pl.BlockSpec(memory_space=pl.ANY)],
            out_specs=pl.BlockSpec((rb, d), lambda b, idx_ref: (b, 0)),
            scratch_shapes=[pltpu.VMEM((n_add, rb, d), F32), pltpu.SemaphoreType.DMA((n_add,))]),
        out_shape=jax.ShapeDtypeStruct((m, d), out_dtype),
        compiler_params=pltpu.CompilerParams(dimension_semantics=("arbitrary",),
                                             vmem_limit_bytes=_vmem_limit((2 * n_add + 4) * rb * d * 4)),
        name=name,
    )(idx, src)


def _moe_mm_body(be_ref, bs_ref, nu_ref, *refs, n_w, has_scale):
    it = iter(refs)
    x_ref = next(it)
    w_refs = [next(it) for _ in range(n_w)]
    scale_ref = next(it) if has_scale else None
    o_ref = next(it)
    wb_refs = [next(it) for _ in range(n_w)]
    del bs_ref
    b = pl.program_id(1)
    valid = b < nu_ref[0]
    changed = jnp.logical_or(b == 0, be_ref[b] != be_ref[jnp.maximum(b - 1, 0)])

    @pl.when(jnp.logical_and(valid, changed))
    def _():
        for w_ref, wb in zip(w_refs, wb_refs):
            wb[...] = w_ref[...].astype(BF16)

    @pl.when(valid)
    def _():
        x = x_ref[...]
        a = jnp.dot(x, wb_refs[0][...], preferred_element_type=F32)
        if n_w == 2:
            a = _silu(a) * jnp.dot(x, wb_refs[1][...], preferred_element_type=F32)
        if has_scale:
            a = a * scale_ref[...]
        o_ref[...] = a.astype(o_ref.dtype)

    @pl.when(jnp.logical_not(valid))
    def _():
        o_ref[...] = jnp.zeros_like(o_ref)


def _moe_mm(x, ws, layer, n_out, tn, rb, group_end, out_dtype, scale=None, name="moe_mm"):
    p, kdim = x.shape
    n_w = len(ws)
    assert MOE_ROWS % rb == 0
    n_exp = group_end.shape[0]
    n_used = (group_end[-1] // rb).astype(jnp.int32)
    blk_src = jnp.minimum(jnp.arange(p // rb, dtype=jnp.int32), n_used - 1)
    blk_expert = jnp.minimum(jnp.sum(group_end[None, :] <= (blk_src * rb)[:, None], axis=1),
                             n_exp - 1).astype(jnp.int32)
    n_used = n_used.reshape(1)
    in_specs = [pl.BlockSpec((rb, kdim), lambda j, b, be, bs, nu: (bs[b], 0))]
    in_specs += [pl.BlockSpec((None, None, kdim, tn), lambda j, b, be, bs, nu: (layer, be[b], 0, j)) for _ in ws]
    args = [x] + list(ws)
    if scale is not None:
        in_specs.append(pl.BlockSpec((rb, 1), lambda j, b, be, bs, nu: (bs[b], 0)))
        args.append(scale)
    osize = jnp.dtype(out_dtype).itemsize
    est = 2 * rb * kdim * 2 + n_w * (2 * kdim * tn * 4 + kdim * tn * 2 + rb * tn * 4) + 2 * rb * tn * osize
    return pl.pallas_call(
        functools.partial(_moe_mm_body, n_w=n_w, has_scale=scale is not None),
        grid_spec=pltpu.PrefetchScalarGridSpec(
            num_scalar_prefetch=3,
            grid=(n_out // tn, p // rb),
            in_specs=in_specs,
            out_specs=pl.BlockSpec((rb, tn), lambda j, b, be, bs, nu: (b, j)),
            scratch_shapes=[pltpu.VMEM((kdim, tn), BF16) for _ in ws]),
        out_shape=jax.ShapeDtypeStruct((p, n_out), out_dtype),
        compiler_params=pltpu.CompilerParams(dimension_semantics=("arbitrary", "arbitrary"),
                                             vmem_limit_bytes=_vmem_limit(est)),
        name=name,
    )(blk_expert, blk_src, n_used, *args)


def _moe_routing(route, n_exp):
    t = route.shape[0]
    rb = MOE_ROWS
    e_flat = jnp.concatenate([route[:, 0], route[:, 1]]).astype(jnp.int32)
    g_flat = jnp.concatenate([route[:, 2], route[:, 3]])
    tok = jnp.concatenate([jnp.arange(t, dtype=jnp.int32)] * TOP_K)
    onehot = (e_flat[:, None] == jnp.arange(n_exp, dtype=jnp.int32)[None, :]).astype(jnp.int32)
    csum = jnp.cumsum(onehot, axis=0)
    rank = jnp.sum(csum * onehot, axis=1) - 1
    counts = csum[-1]
    padded = (counts + rb - 1) // rb * rb
    pend = jnp.cumsum(padded)
    pstart = pend - padded
    ppos = pstart[e_flat] + rank
    p = TOP_K * t + n_exp * rb
    src = jnp.zeros((p,), jnp.int32).at[ppos].set(tok)
    gate = jnp.zeros((p, 1), F32).at[ppos, 0].set(g_flat)
    return src, gate, ppos.astype(jnp.int32), pend.astype(jnp.int32)


def _grid_pos_embed(n_tokens, dim):
    rows = n_tokens // GRID_W
    quarter = dim // 4
    omega = 1.0 / (10000.0 ** (jnp.arange(quarter, dtype=F32) / quarter))

    def axis_embed(n):
        ang = jnp.arange(n, dtype=F32)[:, None] * omega[None, :]
        return jnp.concatenate([jnp.sin(ang), jnp.cos(ang)], axis=-1)

    er = jnp.broadcast_to(axis_embed(rows)[:, None, :], (rows, GRID_W, dim // 2))
    ec = jnp.broadcast_to(axis_embed(GRID_W)[None, :, :], (rows, GRID_W, dim // 2))
    return jnp.concatenate([er, ec], axis=-1).reshape(n_tokens, dim)


def _v3(a):
    return a.reshape(a.shape[0], 1, a.shape[1])


def kernel(x_prompt, x_sample, state_ssd, c, c_ctx, w_mod, b_mod, norm_mix_pre, norm_mix_post, norm_ffn_pre, norm_ffn_post, w_in_even, ssd_conv_w, ssd_conv_b, ssd_dt_bias, ssd_a_log, ssd_d, ssd_norm, cfm_conv_w, cfm_conv_b, cfm_ln_g, cfm_ln_b, w_out_even, w_in_odd, gm_ln_g, gm_ln_b, gm_w_s, gm_b_s, w_out_odd, ffn_w1, ffn_w3, ffn_w2, router_w, moe_w1, moe_w3, moe_w2):
    batch, seq, d = x_prompt.shape
    dec_batch, dec_seq, _ = x_sample.shape
    depth = w_mod.shape[0]
    lay = _Layout(batch, seq, dec_batch, dec_seq)
    tokens = lay.tokens
    inner = d
    heads = inner // SSD_HEAD_DIM
    bc_dim = SSD_GROUPS * SSD_STATE
    conv_dim = inner + 2 * bc_dim
    cfm_dim = cfm_conv_w.shape[2]
    dt_col = inner + conv_dim
    glu_col = dt_col + 2 * heads
    n_exp = router_w.shape[2]
    d_ff = ffn_w1.shape[2]
    d_ffe = moe_w1.shape[3]
    tm = 1024 if tokens % 1024 == 0 else 512
    tm_long = 768 if tokens % 768 == 0 else 512

    n_cond = 1 + dec_batch
    cond = jnp.zeros((8, d), F32).at[0].set(c_ctx).at[1:n_cond].set(c)
    mod = jnp.stack([
        _mm([cond], [(w_mod, (l,), 0)], N_MOD * d, tm=8, tn=512, out_dtype=F32,
            bias=b_mod[l][None, :], x_pro="silu", name="adaln_mod")
        for l in range(depth)])
    mod = mod.reshape(depth, 8, 1, N_MOD * d)

    x = jnp.concatenate([x_prompt.reshape(batch * seq, d), x_sample.reshape(dec_batch * dec_seq, d)], axis=0)
    pos = _grid_pos_embed(dec_seq, d)
    g_mix_pre, g_mix_post = _v3(norm_mix_pre), _v3(norm_mix_post)
    g_ffn_pre, g_ffn_post = _v3(norm_ffn_pre), _v3(norm_ffn_post)
    x, h = _entry(lay, x, pos, g_mix_pre, mod)

    a_neg = -jnp.exp(ssd_a_log.astype(F32))
    h0_all = state_ssd.reshape(dec_batch, state_ssd.shape[1], 2, inner, SSD_STATE)
    dskip = _v3(jnp.repeat(ssd_d, SSD_HEAD_DIM, axis=1))
    glu_w = w_in_even[:, :, glu_col:]
    states = []
    for l in range(depth):
        i = l // 2
        if l % 2 == 0:
            zx = _mm([h], [(w_in_even, (i,), 0)], dt_col, tm=tm, tn=512, name="even_in_zx")
            dt, dtt = _dt_proj(h, w_in_even, i, dt_col, ssd_dt_bias[i].reshape(1, 2 * heads))
            u0 = _mm([h], [(glu_w, (i,), 0), (glu_w, (i,), cfm_dim)], cfm_dim, tm=tm, tn=256,
                     act="glu", name="even_in_glu")
            xa = _dwconv(lay, zx, inner, conv_dim, ssd_conv_w, _v3(ssd_conv_b), i, "silu", BF16)
            a_row = a_neg[i].reshape(1, 2 * heads)
            a_col = jnp.broadcast_to(a_neg[i].reshape(2 * heads, 1), (2 * heads, LANES))
            y2, st = _ssd(lay, xa, dt, dtt, a_row, a_col, h0_all, i, inner)
            states.append(st)
            yg = _ssd_gate(lay, y2, xa, zx, dskip, _v3(ssd_norm), i, inner)
            uc = _dwconv(lay, u0, 0, cfm_dim, cfm_conv_w, _v3(cfm_conv_b), i, "none", F32)
            ug = _ln_silu(lay, uc, _v3(cfm_ln_g), _v3(cfm_ln_b), i)
            y = _mm([yg, ug], [(w_out_even, (i,), 0)], d, tm=tm, tn=256, out_dtype=F32, x_buffers=1,
                    name="even_out")
        else:
            uv = _mm([h], [(w_in_odd, (i,), 0)], w_in_odd.shape[2], tm=tm, tn=512, act="gelu",
                     name="odd_in")
            gated = _gmlp_gate(lay, uv, _v3(gm_ln_g), _v3(gm_ln_b), gm_w_s,
                               jnp.swapaxes(gm_b_s, 1, 2), i)
            y = _mm([gated], [(w_out_odd, (i,), 0)], d, tm=tm, tn=256, out_dtype=F32, x_buffers=1,
                    name="odd_out")

        if l % 2 == 0:
            x, h = _post(lay, x, y, g_mix_post, mod, l, 2, pre=(g_ffn_pre, l, 4, 3))
            act = _mm([h], [(ffn_w1, (i,), 0), (ffn_w3, (i,), 0)], d_ff, tm=tm, tn=256, act="swiglu",
                      name="ffn_up")
            y = _mm([act], [(ffn_w2, (i,), 0)], d, tm=tm_long, tn=256, out_dtype=F32, x_buffers=1,
                    name="ffn_down")
        else:
            rw = jnp.zeros((d, LANES), F32).at[:, :n_exp].set(router_w[i])
            x, h, h32, route = _post(lay, x, y, g_mix_post, mod, l, 2, pre=(g_ffn_pre, l, 4, 3), router=rw,
                                     n_exp=n_exp)
            del h
            src, gate, ppos, group_end = _moe_routing(route, n_exp)
            xs = _row_gather(h32, src, 1, BF16, "moe_gather")
            tn_e = 512 if d_ffe % 512 == 0 else 256
            act = _moe_mm(xs, [moe_w1, moe_w3], i, d_ffe, tn_e, MOE_ROWS, group_end, BF16, name="moe_up")
            ys = _moe_mm(act, [moe_w2], i, d, 512, MOE_DOWN_ROWS, group_end, F32, scale=gate,
                         name="moe_down")
            y = _row_gather(ys, ppos, TOP_K, F32, "moe_combine")

        if l + 1 < depth:
            x, h = _post(lay, x, y, g_ffn_post, mod, l, 5, pre=(g_mix_pre, l + 1, 1, 0))
        else:
            (x,) = _post(lay, x, y, g_ffn_post, mod, l, 5)

    n_ctx = batch * seq
    y_prompt = x[:n_ctx].reshape(batch, seq, d)
    y_sample = x[n_ctx:].reshape(dec_batch, dec_seq, d)
    state_new = jnp.stack(states, axis=1).reshape(batch, len(states), 2, heads, SSD_HEAD_DIM, SSD_STATE)
    return (y_prompt, y_sample, state_new.astype(state_ssd.dtype))
```
